```python
import jax, jax.numpy as jnp
from jax import lax
import numpy as np

D_MODEL = 2048
BATCH = 1
SEQ = 16384
DEPTH = 1

NORM_EPS = 1e-6
SSD_D_INNER = D_MODEL
SSD_HEAD_DIM = 64
SSD_N_HEADS = SSD_D_INNER // SSD_HEAD_DIM
SSD_D_STATE = 128
SSD_N_GROUPS = 8
SSD_CONV_WIDTH = 4
SSD_CONV_DIM = SSD_D_INNER + 2 * SSD_N_GROUPS * SSD_D_STATE
SSD_CHUNK = 128
HG_D = D_MODEL
HG_N_HEADS = 16
HG_K_DIM = HG_D // HG_N_HEADS
HG_V_DIM = HG_D // HG_N_HEADS
HG_KEY_TOTAL = HG_N_HEADS * HG_K_DIM
HG_VAL_TOTAL = HG_N_HEADS * HG_V_DIM
HG_CHUNK = 64
PEER_HEADS = 8
PEER_N_KEYS = 128
PEER_N_EXPERTS = PEER_N_KEYS * PEER_N_KEYS
PEER_TOPK = 16
PEER_D_QUERY = 256
PEER_D_HALF = PEER_D_QUERY // 2
PEER_TOKEN_BLOCK = 128
IN_SIZES = (SSD_D_INNER, SSD_CONV_DIM, SSD_N_HEADS, HG_KEY_TOTAL, HG_KEY_TOTAL, HG_VAL_TOTAL, HG_D, D_MODEL, D_MODEL)
IN_DIM = SSD_D_INNER + SSD_CONV_DIM + SSD_N_HEADS + 2 * HG_KEY_TOTAL + HG_VAL_TOTAL + HG_D + 2 * D_MODEL

kernel_name = 'hybrid_ssd_hgrn2_peer_block'


def _rms(x32):
    return x32 * lax.rsqrt(jnp.mean(jnp.square(x32), axis=-1, keepdims=True) + NORM_EPS)


def rmsnorm(x, w):
    return (_rms(x.astype(jnp.float32)) * w.astype(jnp.float32)).astype(x.dtype)


def group_rmsnorm(y32, w, groups):
    shp = y32.shape
    yg = _rms(y32.reshape(shp[:-1] + (groups, shp[-1] // groups)))
    return yg.reshape(shp) * w.astype(jnp.float32)


def causal_dwconv(u, w, b):
    k, ch = w.shape
    out = lax.conv_general_dilated(u, w[:, None, :], window_strides=(1,), padding=[(k - 1, 0)],
                                   dimension_numbers=('NWC', 'WIO', 'NWC'), feature_group_count=ch)
    return out + b


def segsum(a):
    t = a.shape[-1]
    aa = jnp.broadcast_to(a[..., :, None], a.shape + (t,))
    aa = jnp.where(jnp.tril(jnp.ones((t, t), dtype=bool), -1), aa, 0.0)
    cs = jnp.cumsum(aa, axis=-2)
    return jnp.where(jnp.tril(jnp.ones((t, t), dtype=bool), 0), cs, -jnp.inf)


def ssd_chunked(xdt, adt, bmat, cmat):
    b, l, h, p = xdt.shape
    g, n = bmat.shape[2], bmat.shape[3]
    r = h // g
    c, s = l // SSD_CHUNK, SSD_CHUNK
    xc = xdt.astype(jnp.float32).reshape(b, c, s, g, r, p)
    bc = bmat.astype(jnp.float32).reshape(b, c, s, g, n)
    cc = cmat.astype(jnp.float32).reshape(b, c, s, g, n)
    ac = adt.astype(jnp.float32).reshape(b, c, s, g, r).transpose(0, 3, 4, 1, 2)
    a_cum = jnp.cumsum(ac, axis=-1)
    lmat = jnp.exp(segsum(ac))
    cb = jnp.einsum('bclgn,bcsgn->bgcls', cc, bc)
    y_diag = jnp.einsum('bgrcls,bcsgrp->bclgrp', cb[:, :, None] * lmat, xc)
    decay_states = jnp.exp(a_cum[..., -1:] - a_cum)
    states = jnp.einsum('bcsgn,bgrcs,bcsgrp->bcgrpn', bc, decay_states, xc)
    states = jnp.concatenate([jnp.zeros_like(states[:, :1]), states], axis=1)
    chunk_decay = jnp.exp(segsum(jnp.pad(a_cum[..., -1], ((0, 0), (0, 0), (0, 0), (1, 0)))))
    states = jnp.einsum('bgrzc,bcgrpn->bzgrpn', chunk_decay, states)[:, :-1]
    y_off = jnp.einsum('bclgn,bcgrpn,bgrcl->bclgrp', cc, states, jnp.exp(a_cum))
    return (y_diag + y_off).reshape(b, l, h, p)


def hgrn2_chunked(q, k, v, log_f):
    b, l, h, dk = q.shape
    dv = v.shape[-1]
    c = l // HG_CHUNK

    def to_chunks(a):
        return a.astype(jnp.float32).reshape(b, c, HG_CHUNK, h, a.shape[-1]).transpose(1, 0, 3, 2, 4)

    causal = jnp.tril(jnp.ones((HG_CHUNK, HG_CHUNK), dtype=bool))[:, :, None]

    def step(state, inp):
        qc, kc, vc, gc = inp
        bcum = jnp.cumsum(gc, axis=2)
        diff = bcum[:, :, :, None, :] - bcum[:, :, None, :, :]
        decay = jnp.exp(jnp.where(causal, diff, -jnp.inf))
        scores = jnp.einsum('bhtk,bhsk,bhtsk->bhts', qc, kc, decay)
        o = (jnp.einsum('bhts,bhsv->bhtv', scores, vc)
             + jnp.einsum('bhtk,bhkv->bhtv', qc * jnp.exp(bcum), state))
        b_end = bcum[:, :, -1:, :]
        state = (jnp.exp(b_end[:, :, 0, :, None]) * state
                 + jnp.einsum('bhsk,bhsv->bhkv', kc * jnp.exp(b_end - bcum), vc))
        return state, o

    s0 = jnp.zeros((b, h, dk, dv), jnp.float32)
    _, o = lax.scan(step, s0, (to_chunks(q), to_chunks(k), to_chunks(v), to_chunks(log_f)))
    return o.transpose(1, 0, 3, 2, 4).reshape(b, l, h, dv)


def mixing_block(h, w_in, conv_w, conv_b, dt_bias, a_log, d_skip, ssd_norm_w, lb, hg_norm_w,
                 w_ssd_br, w_hg_br, w_out):
    bsz, seq, _ = h.shape
    f32 = jnp.float32
    split_at = [int(v) for v in np.cumsum(IN_SIZES)[:-1]]
    z, xbc, dt_raw, hq, hf, hi, hg, gate_s, gate_h = jnp.split(h @ w_in, split_at, axis=-1)
    xbc = jax.nn.silu(causal_dwconv(xbc, conv_w, conv_b))
    xs, bm, cm = jnp.split(xbc, [SSD_D_INNER, SSD_D_INNER + SSD_N_GROUPS * SSD_D_STATE], axis=-1)
    xs = xs.astype(f32).reshape(bsz, seq, SSD_N_HEADS, SSD_HEAD_DIM)
    bm = bm.reshape(bsz, seq, SSD_N_GROUPS, SSD_D_STATE)
    cm = cm.reshape(bsz, seq, SSD_N_GROUPS, SSD_D_STATE)
    dt = jax.nn.softplus(dt_raw.astype(f32) + dt_bias.astype(f32))
    a = -jnp.exp(a_log.astype(f32))
    y = ssd_chunked(xs * dt[..., None], a * dt, bm, cm) + d_skip.astype(f32)[:, None] * xs
    y = y.reshape(bsz, seq, SSD_D_INNER) * jax.nn.silu(z.astype(f32))
    y_ssd = group_rmsnorm(y, ssd_norm_w, SSD_N_GROUPS).astype(h.dtype)
    qh = jax.nn.silu(hq.astype(f32)).reshape(bsz, seq, HG_N_HEADS, HG_K_DIM)
    f = lb + (1.0 - lb) * jax.nn.sigmoid(hf.astype(f32))
    kh = (1.0 - f).reshape(bsz, seq, HG_N_HEADS, HG_K_DIM)
    log_f = jnp.log(f).reshape(bsz, seq, HG_N_HEADS, HG_K_DIM)
    vh = hi.reshape(bsz, seq, HG_N_HEADS, HG_V_DIM)
    o = hgrn2_chunked(qh, kh, vh, log_f)
    o = _rms(o) * hg_norm_w.astype(f32) * jax.nn.silu(hg.astype(f32).reshape(bsz, seq, HG_N_HEADS, HG_V_DIM))
    y_hg = o.reshape(bsz, seq, HG_D).astype(h.dtype)
    mix = jax.nn.sigmoid(gate_s) * (y_ssd @ w_ssd_br) + jax.nn.sigmoid(gate_h) * (y_hg @ w_hg_br)
    return mix @ w_out


def peer_ffn(h, w_q, sub_keys, u, v):
    bsz, seq, d = h.shape
    t = bsz * seq
    hf = h.reshape(t, d)
    q = (hf @ w_q).reshape(t, PEER_HEADS, 2, PEER_D_HALF)
    s = jnp.einsum('thpd,hpkd->thpk', q, sub_keys).astype(jnp.float32)
    s1, i1 = lax.top_k(s[:, :, 0], PEER_TOPK)
    s2, i2 = lax.top_k(s[:, :, 1], PEER_TOPK)
    kk = PEER_TOPK * PEER_TOPK
    cand_s = (s1[..., :, None] + s2[..., None, :]).reshape(t, PEER_HEADS, kk)
    cand_i = (i1[..., :, None] * PEER_N_KEYS + i2[..., None, :]).reshape(t, PEER_HEADS, kk)
    top_s, pos = lax.top_k(cand_s, PEER_TOPK)
    idx = jnp.take_along_axis(cand_i, pos, axis=-1)
    gate = jax.nn.softmax(top_s, axis=-1).astype(h.dtype)
    nb = t // PEER_TOKEN_BLOCK
    hk = PEER_HEADS * PEER_TOPK
    xb = hf.reshape(nb, PEER_TOKEN_BLOCK, d)
    ib = idx.reshape(nb, PEER_TOKEN_BLOCK, hk)
    gb = gate.reshape(nb, PEER_TOKEN_BLOCK, hk)

    def expert_block(args):
        xt, it, gt = args
        act = jax.nn.gelu(jnp.einsum('td,tkd->tk', xt, jnp.take(u, it, axis=0)), approximate=False)
        return jnp.einsum('tk,tkd->td', gt * act, jnp.take(v, it, axis=0))

    out = lax.map(expert_block, (xb, ib, gb))
    return out.reshape(bsz, seq, d)


def setup_inputs(seed: int = 0) -> dict:
    key = jax.random.key(seed)
    ks = jax.random.split(key, 24)
    f32 = jnp.float32
    nrm = lambda k, shp, sc: jax.random.normal(k, shp, f32) * sc
    dt0 = jnp.exp(jax.random.uniform(ks[5], (DEPTH, SSD_N_HEADS), f32, np.log(1e-3), np.log(1e-1)))
    return {
        'x': jax.random.normal(ks[0], (BATCH, SEQ, D_MODEL), f32),
        'norm_mix_w': 1.0 + nrm(ks[1], (DEPTH, D_MODEL), 0.02),
        'w_in': nrm(ks[2], (DEPTH, D_MODEL, IN_DIM), D_MODEL ** -0.5),
        'ssd_conv_w': nrm(ks[3], (DEPTH, SSD_CONV_WIDTH, SSD_CONV_DIM), SSD_CONV_WIDTH ** -0.5),
        'ssd_conv_b': nrm(ks[4], (DEPTH, SSD_CONV_DIM), 0.02),
        'ssd_dt_bias': dt0 + jnp.log(-jnp.expm1(-dt0)),
        'ssd_a_log': jnp.log(jax.random.uniform(ks[6], (DEPTH, SSD_N_HEADS), f32, 1.0, 16.0)),
        'ssd_d': 1.0 + nrm(ks[7], (DEPTH, SSD_N_HEADS), 0.1),
        'ssd_norm_w': 1.0 + nrm(ks[8], (DEPTH, SSD_D_INNER), 0.02),
        'hg_lb': nrm(ks[9], (DEPTH + 1, HG_KEY_TOTAL), 0.1),
        'hg_norm_w': 1.0 + nrm(ks[10], (DEPTH, HG_V_DIM), 0.02),
        'w_ssd_br': nrm(ks[11], (DEPTH, SSD_D_INNER, D_MODEL), SSD_D_INNER ** -0.5),
        'w_hg_br': nrm(ks[12], (DEPTH, HG_D, D_MODEL), HG_D ** -0.5),
        'w_out': nrm(ks[13], (DEPTH, D_MODEL, D_MODEL), D_MODEL ** -0.5),
        'norm_ffn_w': 1.0 + nrm(ks[14], (DEPTH, D_MODEL), 0.02),
        'peer_w_q': nrm(ks[15], (DEPTH, D_MODEL, PEER_HEADS * PEER_D_QUERY), D_MODEL ** -0.5),
        'peer_sub_keys': nrm(ks[16], (DEPTH, PEER_HEADS, 2, PEER_N_KEYS, PEER_D_HALF), PEER_D_HALF ** -0.5),
        'peer_u': nrm(ks[17], (DEPTH, PEER_N_EXPERTS, D_MODEL), D_MODEL ** -0.5),
        'peer_v': nrm(ks[18], (DEPTH, PEER_N_EXPERTS, D_MODEL), PEER_HEADS ** -0.5),
        'final_norm_w': 1.0 + nrm(ks[19], (D_MODEL,), 0.02),
    }


def reference(x, norm_mix_w, w_in, ssd_conv_w, ssd_conv_b, ssd_dt_bias, ssd_a_log, ssd_d, ssd_norm_w,
              hg_lb, hg_norm_w, w_ssd_br, w_hg_br, w_out, norm_ffn_w, peer_w_q, peer_sub_keys,
              peer_u, peer_v, final_norm_w):
    lb_all = jnp.cumsum(jax.nn.softmax(hg_lb.astype(jnp.float32), axis=0), axis=0)
    for layer in range(DEPTH):
        h = rmsnorm(x, norm_mix_w[layer])
        x = x + mixing_block(h, w_in[layer], ssd_conv_w[layer], ssd_conv_b[layer], ssd_dt_bias[layer],
                             ssd_a_log[layer], ssd_d[layer], ssd_norm_w[layer], lb_all[layer],
                             hg_norm_w[layer], w_ssd_br[layer], w_hg_br[layer], w_out[layer])
        h = rmsnorm(x, norm_ffn_w[layer])
        x = x + peer_ffn(h, peer_w_q[layer], peer_sub_keys[layer], peer_u[layer], peer_v[layer])
    return rmsnorm(x, final_norm_w)
```

```python
import functools

import jax
import jax.numpy as jnp
from jax import lax
from jax.experimental import pallas as pl
from jax.experimental.pallas import tpu as pltpu

F32 = jnp.float32
BF16 = jnp.bfloat16

D_MODEL = 2048
NORM_EPS = 1e-6
LANES = 128
SUBLANES = 8
VMEM_LIMIT = 48 * 1024 * 1024

SSD_HEAD_DIM = 64
SSD_N_HEADS = 32
SSD_D_STATE = 128
SSD_N_GROUPS = 8
SSD_CONV_WIDTH = 4
SSD_CHUNK = 128
SSD_PAIRS = SSD_N_HEADS // 2
SSD_BC = SSD_N_GROUPS * SSD_D_STATE

HG_N_HEADS = 16
HG_DIM = 128
HG_CHUNK = 128
HG_HEADS_PER_STEP = 4
HG_LEVELS = (64, 32, 16, 8, 4, 2, 1)

PEER_HEADS = 8
PEER_N_KEYS = 128
PEER_TOPK = 16
PEER_D_HALF = 128
PEER_PAIRS = PEER_HEADS * PEER_TOPK
PEER_ROUTE_TOKENS = 128
PEER_TOKEN_BLOCK = 64
PEER_SLOTS = 3

COL_Z, COL_XS, COL_BC, COL_HQ, COL_HF, COL_HI, COL_HG, COL_GS, COL_GH, COL_DT = range(10)
N_COL_BLOCKS = 10


def _params(n_axes, vmem=VMEM_LIMIT):
    return pltpu.CompilerParams(dimension_semantics=("arbitrary",) * n_axes,
                                vmem_limit_bytes=vmem)


def _sigmoid(v):
    return 1.0 / (1.0 + jnp.exp(-v))


def _silu(v):
    return v * _sigmoid(v)


def _split3(v):
    hi = v.astype(BF16)
    r1 = v - hi.astype(F32)
    mid = r1.astype(BF16)
    lo = (r1 - mid.astype(F32)).astype(BF16)
    return hi, mid, lo


def _tri_cumsum(tri_bf16, v):
    hi, mid, lo = _split3(v)
    dot = functools.partial(jnp.dot, preferred_element_type=F32)
    return dot(tri_bf16, hi) + dot(tri_bf16, mid) + dot(tri_bf16, lo)


def _dot_nt(a, b):
    return lax.dot_general(a, b, (((1,), (1,)), ((), ())), preferred_element_type=F32)


def _lower_tri(n):
    row = lax.broadcasted_iota(jnp.int32, (n, n), 0)
    col = lax.broadcasted_iota(jnp.int32, (n, n), 1)
    return row >= col


def _inproj_body(x_ref, nw_ref, w_ref, o_ref, h_scr):
    @pl.when(pl.program_id(1) == 0)
    def _():
        x = x_ref[...]
        ms = jnp.mean(x * x, axis=-1, keepdims=True)
        h_scr[...] = (x * lax.rsqrt(ms + NORM_EPS) * nw_ref[...]).astype(BF16)

    o_ref[...] = jnp.dot(h_scr[...], w_ref[...], preferred_element_type=F32)


def _inproj(x, norm_w, w, tm=1024, tn=1024):
    t, d = x.shape
    n = w.shape[1]
    tm = min(tm, t)
    return pl.pallas_call(
        _inproj_body,
        grid=(t // tm, n // tn),
        in_specs=[pl.BlockSpec((tm, d), lambda i, j: (i, 0)),
                  pl.BlockSpec((1, d), lambda i, j: (0, 0)),
                  pl.BlockSpec((d, tn), lambda i, j: (0, j))],
        out_specs=pl.BlockSpec((tm, tn), lambda i, j: (i, j)),
        out_shape=jax.ShapeDtypeStruct((t, n), F32),
        scratch_shapes=[pltpu.VMEM((tm, d), BF16)],
        compiler_params=_params(2),
        name="inproj",
    )(x, norm_w, w)


def _ssd_body(z_ref, xs_ref, bc_ref, dt_ref, cw_ref, cb_ref, dtb_ref, alog_ref, dsk_ref, nw_ref,
              y_ref, ext_scr, xc_scr, yb_scr, st_scr):
    c = SSD_CHUNK
    halo = SUBLANES

    @pl.when(pl.program_id(0) == 0)
    def _():
        ext_scr[0:halo, :] = jnp.zeros((halo, ext_scr.shape[1]), F32)
        st_scr[...] = jnp.zeros(st_scr.shape, F32)

    ext_scr[halo:halo + c, 0:D_MODEL] = xs_ref[...]
    ext_scr[halo:halo + c, D_MODEL:] = bc_ref[...]

    cblk = 512
    for lo in range(0, ext_scr.shape[1], cblk):
        acc = cb_ref[:, lo:lo + cblk] + cw_ref[SSD_CONV_WIDTH - 1:SSD_CONV_WIDTH, lo:lo + cblk] * \
            ext_scr[halo:halo + c, lo:lo + cblk]
        for j in range(1, SSD_CONV_WIDTH):
            k = SSD_CONV_WIDTH - 1 - j
            acc = acc + cw_ref[k:k + 1, lo:lo + cblk] * ext_scr[halo - j:halo - j + c, lo:lo + cblk]
        xc_scr[:, lo:lo + cblk] = _silu(acc)
    ext_scr[0:halo, :] = ext_scr[c:c + halo, :]

    tri = _lower_tri(c)
    tri_b = tri.astype(BF16)
    lane = lax.broadcasted_iota(jnp.int32, (c, LANES), 1)
    lo_half = lane < SSD_HEAD_DIM

    dt_raw = dt_ref[...] + dtb_ref[...]
    dt = jnp.maximum(dt_raw, 0.0) + jnp.log(1.0 + jnp.exp(-jnp.abs(dt_raw)))
    adt = -jnp.exp(alog_ref[...]) * dt
    acum = _tri_cumsum(tri_b, adt)

    for pr in range(SSD_PAIRS):
        g = pr // 2
        sl = slice(pr * LANES, (pr + 1) * LANES)
        xs_p = xc_scr[:, sl]
        xdt = xs_p * dt[:, sl]
        ac = acum[:, sl]
        ac_t = ac.T
        if pr % 2 == 0:
            bm = xc_scr[:, D_MODEL + g * SSD_D_STATE:D_MODEL + (g + 1) * SSD_D_STATE]
            cm = xc_scr[:, D_MODEL + SSD_BC + g * SSD_D_STATE:D_MODEL + SSD_BC + (g + 1) * SSD_D_STATE]
            cm_b = cm.astype(BF16)
            cb = _dot_nt(cm_b, bm.astype(BF16))
            bm_t = bm.T.astype(BF16)
        gs = []
        for hh in range(2):
            col = jnp.broadcast_to(ac[:, hh * SSD_HEAD_DIM:hh * SSD_HEAD_DIM + 1], (c, c))
            row = ac_t[hh * SSD_HEAD_DIM:hh * SSD_HEAD_DIM + 1, :]
            decay = jnp.exp(jnp.where(tri, col - row, -1e30))
            gs.append(cb * decay)
        gmat = jnp.concatenate(gs, axis=1).astype(BF16)
        x2 = jnp.concatenate([jnp.where(lo_half, xdt, 0.0), jnp.where(lo_half, 0.0, xdt)],
                             axis=0).astype(BF16)
        y_diag = jnp.dot(gmat, x2, preferred_element_type=F32)
        st = st_scr[pr]
        y_off = jnp.dot(cm_b, st.astype(BF16), preferred_element_type=F32) * jnp.exp(ac)
        a_end = ac[c - 1:c, :]
        wgt = (xdt * jnp.exp(a_end - ac)).astype(BF16)
        st_scr[pr] = st * jnp.exp(a_end) + jnp.dot(bm_t, wgt, preferred_element_type=F32)
        y_p = y_diag + y_off + dsk_ref[:, sl] * xs_p
        yb_scr[:, sl] = y_p * _silu(z_ref[:, sl])

    gw = D_MODEL // SSD_N_GROUPS
    for g in range(SSD_N_GROUPS):
        yg = yb_scr[:, g * gw:(g + 1) * gw]
        ms = jnp.mean(yg * yg, axis=-1, keepdims=True)
        y_ref[:, g * gw:(g + 1) * gw] = (yg * lax.rsqrt(ms + NORM_EPS) * nw_ref[:, g * gw:(g + 1) * gw]).astype(BF16)


def _ssd(main, conv_w, conv_b, dtb_x, alog_x, dskip_x, norm_w):
    t = main.shape[0]
    c = SSD_CHUNK
    conv_dim = conv_w.shape[1]

    def col(b):
        return pl.BlockSpec((c, D_MODEL), lambda i, b=b: (i, b))

    def full(shape):
        return pl.BlockSpec(shape, lambda i: (0,) * len(shape))

    return pl.pallas_call(
        _ssd_body,
        grid=(t // c,),
        in_specs=[col(COL_Z), col(COL_XS), col(COL_BC), col(COL_DT),
                  full((SSD_CONV_WIDTH, conv_dim)), full((1, conv_dim)),
                  full((1, D_MODEL)), full((1, D_MODEL)), full((1, D_MODEL)), full((1, D_MODEL))],
        out_specs=pl.BlockSpec((c, D_MODEL), lambda i: (i, 0)),
        out_shape=jax.ShapeDtypeStruct((t, D_MODEL), BF16),
        scratch_shapes=[pltpu.VMEM((c + 2 * SUBLANES, conv_dim), F32),
                        pltpu.VMEM((c, conv_dim), F32),
                        pltpu.VMEM((c, D_MODEL), F32),
                        pltpu.VMEM((SSD_PAIRS, SSD_D_STATE, LANES), F32)],
        compiler_params=_params(1),
        name="ssd",
    )(main, main, main, main, conv_w, conv_b, dtb_x, alog_x, dskip_x, norm_w)


def _hg_anchor(b, blk):
    n = b.shape[0]
    if blk >= 4:
        pieces = []
        for m in range(n // (2 * blk)):
            r = (2 * m + 1) * blk - 1
            pieces.append(jnp.broadcast_to(b[r:r + 1, :], (2 * blk, b.shape[1])))
        return pieces[0] if len(pieces) == 1 else jnp.concatenate(pieces, axis=0)
    pos = lax.broadcasted_iota(jnp.int32, b.shape, 0) & (2 * blk - 1)
    out = b
    for j in range(2 * blk):
        shift = j - (blk - 1)
        if shift != 0:
            out = jnp.where(pos == j, pltpu.roll(b, shift % n, 0), out)
    return out


def _hgrn_body(hq_ref, hf_ref, hi_ref, hg_ref, lb_ref, nw_ref, y_ref, st_scr):
    c = HG_CHUNK

    @pl.when(pl.program_id(1) == 0)
    def _():
        st_scr[...] = jnp.zeros(st_scr.shape, F32)

    tri_b = _lower_tri(c).astype(BF16)
    row = lax.broadcasted_iota(jnp.int32, (c, c), 0)
    col = lax.broadcasted_iota(jnp.int32, (c, c), 1)
    level_masks = []
    for blk in HG_LEVELS:
        sh = blk.bit_length() - 1
        tb = row >> sh
        level_masks.append(((tb & 1) == 1, ((tb & 1) == 1) & ((col >> sh) == tb - 1)))

    lbr = lb_ref[...]
    e = jnp.exp(lbr - jnp.max(lbr, axis=0, keepdims=True))
    lb_all = e[0:1, :] / jnp.sum(e, axis=0, keepdims=True)

    for h in range(HG_HEADS_PER_STEP):
        sl = slice(h * HG_DIM, (h + 1) * HG_DIM)
        q = _silu(hq_ref[:, sl])
        lb = lb_all[:, sl]
        f = lb + (1.0 - lb) * _sigmoid(hf_ref[:, sl])
        k = 1.0 - f
        v = hi_ref[:, sl]
        v_b = v.astype(BF16)
        b = _tri_cumsum(tri_b, jnp.log(f))

        a = jnp.zeros((c, c), F32)
        for blk, (is_tgt, pair_mask) in zip(HG_LEVELS, level_masks):
            d = b - _hg_anchor(b, blk)
            xl = (jnp.where(is_tgt, q, k) * jnp.exp(jnp.where(is_tgt, d, -d))).astype(BF16)
            a = a + jnp.where(pair_mask, _dot_nt(xl, xl), 0.0)
        o = jnp.dot(a.astype(BF16), v_b, preferred_element_type=F32)
        o = o + jnp.sum(q * k, axis=-1, keepdims=True) * v

        st = st_scr[h]
        o = o + _dot_nt((q * jnp.exp(b)).astype(BF16), st.astype(BF16))
        b_end = b[c - 1:c, :]
        ke = (k * jnp.exp(b_end - b)).astype(BF16)
        st_scr[h] = st * jnp.exp(b_end) + jnp.dot(v.T.astype(BF16), ke, preferred_element_type=F32)

        ms = jnp.mean(o * o, axis=-1, keepdims=True)
        y_ref[:, sl] = (o * lax.rsqrt(ms + NORM_EPS) * nw_ref[...] * _silu(hg_ref[:, sl])).astype(BF16)


def _hgrn(main, hg_lb, hg_norm_w):
    t = main.shape[0]
    c = HG_CHUNK
    w = HG_HEADS_PER_STEP * HG_DIM
    per = D_MODEL // w

    def col(b):
        return pl.BlockSpec((c, w), lambda g, i, b=b: (i, b * per + g))

    return pl.pallas_call(
        _hgrn_body,
        grid=(per, t // c),
        in_specs=[col(COL_HQ), col(COL_HF), col(COL_HI), col(COL_HG),
                  pl.BlockSpec((hg_lb.shape[0], w), lambda g, i: (0, g)),
                  pl.BlockSpec((1, HG_DIM), lambda g, i: (0, 0))],
        out_specs=pl.BlockSpec((c, w), lambda g, i: (i, g)),
        out_shape=jax.ShapeDtypeStruct((t, D_MODEL), BF16),
        scratch_shapes=[pltpu.VMEM((HG_HEADS_PER_STEP, HG_DIM, HG_DIM), F32)],
        compiler_params=_params(2),
        name="hgrn",
    )(main, main, main, main, hg_lb, hg_norm_w)


def _merge_body(ys_ref, yh_ref, ws_ref, wh_ref, gs_ref, gh_ref, o_ref):
    ps = jnp.dot(ys_ref[...], ws_ref[...], preferred_element_type=F32)
    ph = jnp.dot(yh_ref[...], wh_ref[...], preferred_element_type=F32)
    o_ref[...] = (_sigmoid(gs_ref[...]) * ps + _sigmoid(gh_ref[...]) * ph).astype(BF16)


def _merge(y_ssd, y_hg, w_s, w_h, main, tm=512, tn=1024):
    t = y_ssd.shape[0]
    tm = min(tm, t)
    per = D_MODEL // tn
    return pl.pallas_call(
        _merge_body,
        grid=(t // tm, D_MODEL // tn),
        in_specs=[pl.BlockSpec((tm, D_MODEL), lambda i, j: (i, 0)),
                  pl.BlockSpec((tm, D_MODEL), lambda i, j: (i, 0)),
                  pl.BlockSpec((D_MODEL, tn), lambda i, j: (0, j)),
                  pl.BlockSpec((D_MODEL, tn), lambda i, j: (0, j)),
                  pl.BlockSpec((tm, tn), lambda i, j: (i, COL_GS * per + j)),
                  pl.BlockSpec((tm, tn), lambda i, j: (i, COL_GH * per + j))],
        out_specs=pl.BlockSpec((tm, tn), lambda i, j: (i, j)),
        out_shape=jax.ShapeDtypeStruct((t, D_MODEL), BF16),
        compiler_params=_params(2),
        name="merge",
    )(y_ssd, y_hg, w_s, w_h, main, main)


def _outproj_body(mix_ref, w_ref, x_ref, nw_ref, x1_ref, h_ref, hb_ref):
    x1 = x_ref[...] + jnp.dot(mix_ref[...], w_ref[...], preferred_element_type=F32)
    x1_ref[...] = x1
    ms = jnp.mean(x1 * x1, axis=-1, keepdims=True)
    h = x1 * lax.rsqrt(ms + NORM_EPS) * nw_ref[...]
    h_ref[...] = h
    hb_ref[...] = h.astype(BF16)


def _outproj(mix, w_out, x, norm_w, tm=512):
    t = x.shape[0]
    tm = min(tm, t)
    row = pl.BlockSpec((tm, D_MODEL), lambda i: (i, 0))
    return pl.pallas_call(
        _outproj_body,
        grid=(t // tm,),
        in_specs=[row, pl.BlockSpec((D_MODEL, D_MODEL), lambda i: (0, 0)), row,
                  pl.BlockSpec((1, D_MODEL), lambda i: (0, 0))],
        out_specs=[row, row, row],
        out_shape=[jax.ShapeDtypeStruct((t, D_MODEL), F32),
                   jax.ShapeDtypeStruct((t, D_MODEL), F32),
                   jax.ShapeDtypeStruct((t, D_MODEL), BF16)],
        compiler_params=_params(1),
        name="outproj",
    )(mix, w_out, x, norm_w)


def _matmul_body(a_ref, b_ref, o_ref):
    o_ref[...] = jnp.dot(a_ref[...], b_ref[...], preferred_element_type=F32)


def _matmul(a, b, tm=512, tn=1024):
    m, k = a.shape
    n = b.shape[1]
    tm = min(tm, m)
    return pl.pallas_call(
        _matmul_body,
        grid=(m // tm, n // tn),
        in_specs=[pl.BlockSpec((tm, k), lambda i, j: (i, 0)),
                  pl.BlockSpec((k, tn), lambda i, j: (0, j))],
        out_specs=pl.BlockSpec((tm, tn), lambda i, j: (i, j)),
        out_shape=jax.ShapeDtypeStruct((m, n), F32),
        compiler_params=_params(2),
        name="peer_query",
    )(a, b)


def _topk_rows(s, k, on_pick):
    n = s.shape[0]
    rows = lax.broadcasted_iota(jnp.int32, s.shape, 0)
    for j in range(k):
        m = jnp.max(s, axis=0, keepdims=True)
        i = jnp.min(jnp.where(s == m, rows, n), axis=0, keepdims=True)
        hit = rows == i
        on_pick(j, m, i, hit)
        s = jnp.where(hit, -jnp.inf, s)


def _route_body(q_ref, keys_ref, idx_ref, gate_ref, s2_scr, i2_scr, ts_scr):
    tb = q_ref.shape[0]
    k = PEER_TOPK
    q = q_ref[...]
    hi = lax.Precision.HIGHEST
    s_a = lax.dot_general(keys_ref[0, 0], q[:, :PEER_D_HALF], (((1,), (1,)), ((), ())),
                          precision=hi, preferred_element_type=F32)
    s_b = lax.dot_general(keys_ref[0, 1], q[:, PEER_D_HALF:], (((1,), (1,)), ((), ())),
                          precision=hi, preferred_element_type=F32)

    s1, i1 = [None] * k, [None] * k

    def pick_a(j, m, i, hit):
        s1[j], i1[j] = m, i

    def pick_b(j, m, i, hit):
        s2_scr[j:j + 1, :] = m
        i2_scr[j:j + 1, :] = i

    _topk_rows(s_a, k, pick_a)
    _topk_rows(s_b, k, pick_b)
    s2 = s2_scr[...]
    i2 = i2_scr[...]
    cand_s = jnp.concatenate([s1[a] + s2 for a in range(k)], axis=0)
    cand_i = jnp.concatenate([i1[a] * PEER_N_KEYS + i2 for a in range(k)], axis=0)

    def pick_c(j, m, i, hit):
        ts_scr[j:j + 1, :] = m
        idx_ref[j:j + 1, :] = jnp.max(jnp.where(hit, cand_i, -1), axis=0, keepdims=True)

    _topk_rows(cand_s, k, pick_c)
    ts = ts_scr[...]
    e = jnp.exp(ts - ts[0:1, :])
    gate_ref[...] = e / jnp.sum(e, axis=0, keepdims=True)


def _route(q, sub_keys, tb=PEER_ROUTE_TOKENS):
    t = q.shape[0]
    k = PEER_TOPK
    out_spec = pl.BlockSpec((k, tb), lambda i, h: (h, i))
    return pl.pallas_call(
        _route_body,
        grid=(t // tb, PEER_HEADS),
        in_specs=[pl.BlockSpec((tb, 2 * PEER_D_HALF), lambda i, h: (i, h)),
                  pl.BlockSpec((1, 2, PEER_N_KEYS, PEER_D_HALF), lambda i, h: (h, 0, 0, 0))],
        out_specs=[out_spec, out_spec],
        out_shape=[jax.ShapeDtypeStruct((PEER_PAIRS, t), jnp.int32),
                   jax.ShapeDtypeStruct((PEER_PAIRS, t), F32)],
        scratch_shapes=[pltpu.VMEM((k, tb), F32), pltpu.VMEM((k, tb), jnp.int32),
                        pltpu.VMEM((k, tb), F32)],
        compiler_params=_params(2),
        name="peer_route",
    )(q, sub_keys)


def _peer_body(idx_ref, h_ref, x1_ref, gate_ref, fw_ref, tab_ref, o_ref, buf, acc_scr, sem):
    tb = h_ref.shape[0]
    n_lane_tiles = D_MODEL // LANES

    def row_copy(tok, r, slot):
        e = idx_ref[tok, r]
        return pltpu.make_async_copy(tab_ref.at[pl.ds(e, 1)], buf.at[slot, pl.ds(r, 1)], sem.at[slot])

    def start_token(tok, slot):
        for r in range(PEER_PAIRS):
            row_copy(tok, r, slot).start()

    def wait_token(slot):
        pltpu.make_async_copy(tab_ref.at[pl.ds(0, PEER_PAIRS)], buf.at[slot], sem.at[slot]).wait()

    for p in range(PEER_SLOTS - 1):
        start_token(p, p)

    eye = (lax.broadcasted_iota(jnp.int32, (PEER_PAIRS, PEER_PAIRS), 0)
           == lax.broadcasted_iota(jnp.int32, (PEER_PAIRS, PEER_PAIRS), 1))
    hi_mask = jnp.uint32(0xFFFF0000)
    sub_iota = lax.broadcasted_iota(jnp.int32, (SUBLANES, LANES), 0)
    acc_scr[...] = jnp.zeros(acc_scr.shape, F32)

    def token(tok, carry):
        slot = tok % PEER_SLOTS

        @pl.when(tok + PEER_SLOTS - 1 < tb)
        def _():
            nxt = tok + PEER_SLOTS - 1
            start_token(nxt, nxt % PEER_SLOTS)

        wait_token(slot)
        hrow = h_ref[pl.ds(tok, 1), :]
        acc = jnp.zeros((PEER_PAIRS, LANES), F32)
        for j in range(n_lane_tiles):
            w = buf[slot, :, j * LANES:(j + 1) * LANES]
            u = lax.bitcast_convert_type(w & hi_mask, F32)
            acc = acc + u * hrow[:, j * LANES:(j + 1) * LANES]
        act = jnp.sum(acc, axis=-1, keepdims=True)
        act = 0.5 * act * (1.0 + lax.erf(act * (2.0 ** -0.5)))
        grow = jnp.broadcast_to(gate_ref[pl.ds(tok, 1), :], (PEER_PAIRS, PEER_PAIRS))
        g = jnp.sum(jnp.where(eye, grow, 0.0), axis=-1, keepdims=True)
        wcol = jnp.broadcast_to(act * g, (PEER_PAIRS, LANES))
        base = pl.multiple_of((tok // SUBLANES) * SUBLANES, SUBLANES)
        mine = sub_iota == tok % SUBLANES
        for j in range(n_lane_tiles):
            cols = slice(j * LANES, (j + 1) * LANES)
            v = lax.bitcast_convert_type(buf[slot, :, cols] << 16, F32)
            orow = jnp.sum(v * wcol, axis=0, keepdims=True)
            tile = acc_scr[pl.ds(base, SUBLANES), cols]
            acc_scr[pl.ds(base, SUBLANES), cols] = jnp.where(mine, orow, tile)
        return carry

    lax.fori_loop(0, tb, token, 0)

    xo = x1_ref[...] + acc_scr[...]
    ms = jnp.mean(xo * xo, axis=-1, keepdims=True)
    o_ref[...] = xo * lax.rsqrt(ms + NORM_EPS) * fw_ref[...]


def _peer(idx, h, x1, gate, final_w, table, tb=PEER_TOKEN_BLOCK):
    t = h.shape[0]
    row = pl.BlockSpec((tb, D_MODEL), lambda i: (i, 0))
    return pl.pallas_call(
        _peer_body,
        grid=(t // tb,),
        in_specs=[pl.BlockSpec((tb, PEER_PAIRS), lambda i: (i, 0), memory_space=pltpu.SMEM),
                  row, row,
                  pl.BlockSpec((tb, PEER_PAIRS), lambda i: (i, 0)),
                  pl.BlockSpec((1, D_MODEL), lambda i: (0, 0)),
                  pl.BlockSpec(memory_space=pl.ANY)],
        out_specs=row,
        out_shape=jax.ShapeDtypeStruct((t, D_MODEL), F32),
        scratch_shapes=[pltpu.VMEM((PEER_SLOTS, PEER_PAIRS, D_MODEL), jnp.uint32),
                        pltpu.VMEM((tb, D_MODEL), F32),
                        pltpu.SemaphoreType.DMA((PEER_SLOTS,))],
        compiler_params=_params(1),
        name="peer_ffn",
    )(idx, h, x1, gate, final_w, table)


def _expand_heads(v, width):
    return jnp.repeat(v.astype(F32), width)[None, :]


def _layer(x, norm_mix_w, w_in, conv_w, conv_b, dt_bias, a_log, d_skip, ssd_norm_w, hg_lb, hg_norm_w,
           w_ssd_br, w_hg_br, w_out, norm_ffn_w, w_q, sub_keys, peer_u, peer_v, out_norm_w):
    d = D_MODEL
    conv_dim = conv_w.shape[1]
    o_dt = d + conv_dim
    w_dt = jnp.repeat(w_in[:, o_dt:o_dt + SSD_N_HEADS], SSD_HEAD_DIM, axis=1)
    w_main = jnp.concatenate([w_in[:, :o_dt], w_in[:, o_dt + SSD_N_HEADS:], w_dt], axis=1).astype(BF16)

    main = _inproj(x, norm_mix_w[None, :], w_main)
    y_ssd = _ssd(main, conv_w, conv_b[None, :], _expand_heads(dt_bias, SSD_HEAD_DIM),
                 _expand_heads(a_log, SSD_HEAD_DIM), _expand_heads(d_skip, SSD_HEAD_DIM),
                 ssd_norm_w[None, :])
    y_hg = _hgrn(main, hg_lb, hg_norm_w[None, :])
    mix = _merge(y_ssd, y_hg, w_ssd_br.astype(BF16), w_hg_br.astype(BF16), main)
    x1, h2, h2b = _outproj(mix, w_out.astype(BF16), x, norm_ffn_w[None, :])

    q = _matmul(h2b, w_q.astype(BF16))
    idx_t, gate_t = _route(q, sub_keys)
    ub = lax.bitcast_convert_type(peer_u.astype(BF16), jnp.uint16).astype(jnp.uint32)
    vb = lax.bitcast_convert_type(peer_v.astype(BF16), jnp.uint16).astype(jnp.uint32)
    table = (ub << 16) | vb
    return _peer(idx_t.T, h2, x1, gate_t.T, out_norm_w[None, :], table)


def kernel(x, norm_mix_w, w_in, ssd_conv_w, ssd_conv_b, ssd_dt_bias, ssd_a_log, ssd_d, ssd_norm_w, hg_lb,
           hg_norm_w, w_ssd_br, w_hg_br, w_out, norm_ffn_w, peer_w_q, peer_sub_keys, peer_u, peer_v,
           final_norm_w):
    bsz, seq, d = x.shape
    depth = w_in.shape[0]
    assert bsz == 1 and depth == 1 and d == D_MODEL
    out = _layer(x.reshape(seq, d), norm_mix_w[0], w_in[0], ssd_conv_w[0], ssd_conv_b[0], ssd_dt_bias[0],
                 ssd_a_log[0], ssd_d[0], ssd_norm_w[0], hg_lb, hg_norm_w[0], w_ssd_br[0], w_hg_br[0],
                 w_out[0], norm_ffn_w[0], peer_w_q[0], peer_sub_keys[0], peer_u[0], peer_v[0], final_norm_w)
    return out.reshape(bsz, seq, d)
```

```python
import functools

import jax
import jax.numpy as jnp
from jax import lax
from jax.experimental import pallas as pl
from jax.experimental.pallas import tpu as pltpu

F32 = jnp.float32
BF16 = jnp.bfloat16

D_MODEL = 2048
NORM_EPS = 1e-6
LANES = 128
SUBLANES = 8
VMEM_LIMIT = 48 * 1024 * 1024

SSD_HEAD_DIM = 64
SSD_N_HEADS = 32
SSD_D_STATE = 128
SSD_N_GROUPS = 8
SSD_CONV_WIDTH = 4
SSD_CHUNK = 128
SSD_PAIRS = SSD_N_HEADS // 2
SSD_BC = SSD_N_GROUPS * SSD_D_STATE

HG_N_HEADS = 16
HG_DIM = 128
HG_CHUNK = 128
HG_HEADS_PER_STEP = 4
HG_LEVELS = (64, 32, 16, 8, 4, 2, 1)

PEER_HEADS = 8
PEER_N_KEYS = 128
PEER_TOPK = 16
PEER_D_HALF = 128
PEER_PAIRS = PEER_HEADS * PEER_TOPK
PEER_ROUTE_TOKENS = 512
PEER_CAND_ROWS = -(-sum(PEER_TOPK // (a + 1) for a in range(PEER_TOPK)) // SUBLANES) * SUBLANES
PEER_TOKEN_BLOCK = 64
PEER_SLOTS = 8
PEER_AHEAD = 6

COL_Z, COL_XS, COL_BC, COL_HQ, COL_HF, COL_HI, COL_HG, COL_GS, COL_GH, COL_DT = range(10)
N_COL_BLOCKS = 10


def _params(n_axes, vmem=VMEM_LIMIT, flags=None):
    return pltpu.CompilerParams(dimension_semantics=("arbitrary",) * n_axes,
                                vmem_limit_bytes=vmem, flags=flags)


def _sigmoid(v):
    return 1.0 / (1.0 + jnp.exp(-v))


def _silu(v):
    return v * _sigmoid(v)


def _split3(v):
    hi = v.astype(BF16)
    r1 = v - hi.astype(F32)
    mid = r1.astype(BF16)
    lo = (r1 - mid.astype(F32)).astype(BF16)
    return hi, mid, lo


def _tri_cumsum(tri_bf16, v):
    hi, mid, lo = _split3(v)
    dot = functools.partial(jnp.dot, preferred_element_type=F32)
    return dot(tri_bf16, hi) + dot(tri_bf16, mid) + dot(tri_bf16, lo)


def _dot_nt(a, b):
    return lax.dot_general(a, b, (((1,), (1,)), ((), ())), preferred_element_type=F32)


def _lower_tri(n):
    row = lax.broadcasted_iota(jnp.int32, (n, n), 0)
    col = lax.broadcasted_iota(jnp.int32, (n, n), 1)
    return row >= col


def _inproj_body(x_ref, nw_ref, w_ref, o_ref, h_scr):
    @pl.when(pl.program_id(1) == 0)
    def _():
        x = x_ref[...]
        ms = jnp.mean(x * x, axis=-1, keepdims=True)
        h_scr[...] = (x * lax.rsqrt(ms + NORM_EPS) * nw_ref[...]).astype(BF16)

    o_ref[...] = jnp.dot(h_scr[...], w_ref[...], preferred_element_type=F32)


def _inproj(x, norm_w, w, tm=1024, tn=1024):
    t, d = x.shape
    n = w.shape[1]
    tm = min(tm, t)
    return pl.pallas_call(
        _inproj_body,
        grid=(t // tm, n // tn),
        in_specs=[pl.BlockSpec((tm, d), lambda i, j: (i, 0)),
                  pl.BlockSpec((1, d), lambda i, j: (0, 0)),
                  pl.BlockSpec((d, tn), lambda i, j: (0, j))],
        out_specs=pl.BlockSpec((tm, tn), lambda i, j: (i, j)),
        out_shape=jax.ShapeDtypeStruct((t, n), F32),
        scratch_shapes=[pltpu.VMEM((tm, d), BF16)],
        compiler_params=_params(2),
        name="inproj",
    )(x, norm_w, w)


def _ssd_body(z_ref, xs_ref, bc_ref, dt_ref, cw_ref, cb_ref, dtb_ref, alog_ref, dsk_ref, nw_ref,
              y_ref, ext_scr, xc_scr, yb_scr, st_scr):
    c = SSD_CHUNK
    halo = SUBLANES

    @pl.when(pl.program_id(0) == 0)
    def _():
        ext_scr[0:halo, :] = jnp.zeros((halo, ext_scr.shape[1]), F32)
        st_scr[...] = jnp.zeros(st_scr.shape, F32)

    ext_scr[halo:halo + c, 0:D_MODEL] = xs_ref[...]
    ext_scr[halo:halo + c, D_MODEL:] = bc_ref[...]

    cblk = 512
    for lo in range(0, ext_scr.shape[1], cblk):
        acc = cb_ref[:, lo:lo + cblk] + cw_ref[SSD_CONV_WIDTH - 1:SSD_CONV_WIDTH, lo:lo + cblk] * \
            ext_scr[halo:halo + c, lo:lo + cblk]
        for j in range(1, SSD_CONV_WIDTH):
            k = SSD_CONV_WIDTH - 1 - j
            acc = acc + cw_ref[k:k + 1, lo:lo + cblk] * ext_scr[halo - j:halo - j + c, lo:lo + cblk]
        xc_scr[:, lo:lo + cblk] = _silu(acc)
    ext_scr[0:halo, :] = ext_scr[c:c + halo, :]

    tri = _lower_tri(c)
    tri_b = tri.astype(BF16)
    lane = lax.broadcasted_iota(jnp.int32, (c, LANES), 1)
    lo_half = lane < SSD_HEAD_DIM

    dt_raw = dt_ref[...] + dtb_ref[...]
    dt = jnp.maximum(dt_raw, 0.0) + jnp.log(1.0 + jnp.exp(-jnp.abs(dt_raw)))
    adt = -jnp.exp(alog_ref[...]) * dt
    acum = _tri_cumsum(tri_b, adt)

    for pr in range(SSD_PAIRS):
        g = pr // 2
        sl = slice(pr * LANES, (pr + 1) * LANES)
        xs_p = xc_scr[:, sl]
        xdt = xs_p * dt[:, sl]
        ac = acum[:, sl]
        ac_t = ac.T
        if pr % 2 == 0:
            bm = xc_scr[:, D_MODEL + g * SSD_D_STATE:D_MODEL + (g + 1) * SSD_D_STATE]
            cm = xc_scr[:, D_MODEL + SSD_BC + g * SSD_D_STATE:D_MODEL + SSD_BC + (g + 1) * SSD_D_STATE]
            cm_b = cm.astype(BF16)
            cb = _dot_nt(cm_b, bm.astype(BF16))
            bm_t = bm.T.astype(BF16)
        gs = []
        for hh in range(2):
            col = jnp.broadcast_to(ac[:, hh * SSD_HEAD_DIM:hh * SSD_HEAD_DIM + 1], (c, c))
            row = ac_t[hh * SSD_HEAD_DIM:hh * SSD_HEAD_DIM + 1, :]
            decay = jnp.exp(jnp.where(tri, col - row, -1e30))
            gs.append(cb * decay)
        gmat = jnp.concatenate(gs, axis=1).astype(BF16)
        x2 = jnp.concatenate([jnp.where(lo_half, xdt, 0.0), jnp.where(lo_half, 0.0, xdt)],
                             axis=0).astype(BF16)
        y_diag = jnp.dot(gmat, x2, preferred_element_type=F32)
        st = st_scr[pr]
        y_off = jnp.dot(cm_b, st.astype(BF16), preferred_element_type=F32) * jnp.exp(ac)
        a_end = ac[c - 1:c, :]
        wgt = (xdt * jnp.exp(a_end - ac)).astype(BF16)
        st_scr[pr] = st * jnp.exp(a_end) + jnp.dot(bm_t, wgt, preferred_element_type=F32)
        y_p = y_diag + y_off + dsk_ref[:, sl] * xs_p
        yb_scr[:, sl] = y_p * _silu(z_ref[:, sl])

    gw = D_MODEL // SSD_N_GROUPS
    for g in range(SSD_N_GROUPS):
        yg = yb_scr[:, g * gw:(g + 1) * gw]
        ms = jnp.mean(yg * yg, axis=-1, keepdims=True)
        y_ref[:, g * gw:(g + 1) * gw] = (yg * lax.rsqrt(ms + NORM_EPS) * nw_ref[:, g * gw:(g + 1) * gw]).astype(BF16)


def _ssd(main, conv_w, conv_b, dtb_x, alog_x, dskip_x, norm_w):
    t = main.shape[0]
    c = SSD_CHUNK
    conv_dim = conv_w.shape[1]

    def col(b):
        return pl.BlockSpec((c, D_MODEL), lambda i, b=b: (i, b))

    def full(shape):
        return pl.BlockSpec(shape, lambda i: (0,) * len(shape))

    return pl.pallas_call(
        _ssd_body,
        grid=(t // c,),
        in_specs=[col(COL_Z), col(COL_XS), col(COL_BC), col(COL_DT),
                  full((SSD_CONV_WIDTH, conv_dim)), full((1, conv_dim)),
                  full((1, D_MODEL)), full((1, D_MODEL)), full((1, D_MODEL)), full((1, D_MODEL))],
        out_specs=pl.BlockSpec((c, D_MODEL), lambda i: (i, 0)),
        out_shape=jax.ShapeDtypeStruct((t, D_MODEL), BF16),
        scratch_shapes=[pltpu.VMEM((c + 2 * SUBLANES, conv_dim), F32),
                        pltpu.VMEM((c, conv_dim), F32),
                        pltpu.VMEM((c, D_MODEL), F32),
                        pltpu.VMEM((SSD_PAIRS, SSD_D_STATE, LANES), F32)],
        compiler_params=_params(1),
        name="ssd",
    )(main, main, main, main, conv_w, conv_b, dtb_x, alog_x, dskip_x, norm_w)


def _hg_anchor(b, blk):
    n = b.shape[0]
    if blk >= 4:
        pieces = []
        for m in range(n // (2 * blk)):
            r = (2 * m + 1) * blk - 1
            pieces.append(jnp.broadcast_to(b[r:r + 1, :], (2 * blk, b.shape[1])))
        return pieces[0] if len(pieces) == 1 else jnp.concatenate(pieces, axis=0)
    pos = lax.broadcasted_iota(jnp.int32, b.shape, 0) & (2 * blk - 1)
    out = b
    for j in range(2 * blk):
        shift = j - (blk - 1)
        if shift != 0:
            out = jnp.where(pos == j, pltpu.roll(b, shift % n, 0), out)
    return out


def _hgrn_body(hq_ref, hf_ref, hi_ref, hg_ref, lb_ref, nw_ref, y_ref, st_scr):
    c = HG_CHUNK

    @pl.when(pl.program_id(1) == 0)
    def _():
        st_scr[...] = jnp.zeros(st_scr.shape, F32)

    tri_b = _lower_tri(c).astype(BF16)
    row = lax.broadcasted_iota(jnp.int32, (c, c), 0)
    col = lax.broadcasted_iota(jnp.int32, (c, c), 1)
    level_masks = []
    for blk in HG_LEVELS:
        sh = blk.bit_length() - 1
        tb = row >> sh
        level_masks.append(((tb & 1) == 1, ((tb & 1) == 1) & ((col >> sh) == tb - 1)))

    lbr = lb_ref[...]
    e = jnp.exp(lbr - jnp.max(lbr, axis=0, keepdims=True))
    lb_all = e[0:1, :] / jnp.sum(e, axis=0, keepdims=True)

    for h in range(HG_HEADS_PER_STEP):
        sl = slice(h * HG_DIM, (h + 1) * HG_DIM)
        q = _silu(hq_ref[:, sl])
        lb = lb_all[:, sl]
        f = lb + (1.0 - lb) * _sigmoid(hf_ref[:, sl])
        k = 1.0 - f
        v = hi_ref[:, sl]
        v_b = v.astype(BF16)
        b = _tri_cumsum(tri_b, jnp.log(f))

        a = jnp.zeros((c, c), F32)
        for blk, (is_tgt, pair_mask) in zip(HG_LEVELS, level_masks):
            d = b - _hg_anchor(b, blk)
            xl = (jnp.where(is_tgt, q, k) * jnp.exp(jnp.where(is_tgt, d, -d))).astype(BF16)
            a = a + jnp.where(pair_mask, _dot_nt(xl, xl), 0.0)
        o = jnp.dot(a.astype(BF16), v_b, preferred_element_type=F32)
        o = o + jnp.sum(q * k, axis=-1, keepdims=True) * v

        st = st_scr[h]
        o = o + _dot_nt((q * jnp.exp(b)).astype(BF16), st.astype(BF16))
        b_end = b[c - 1:c, :]
        ke = (k * jnp.exp(b_end - b)).astype(BF16)
        st_scr[h] = st * jnp.exp(b_end) + jnp.dot(v.T.astype(BF16), ke, preferred_element_type=F32)

        ms = jnp.mean(o * o, axis=-1, keepdims=True)
        y_ref[:, sl] = (o * lax.rsqrt(ms + NORM_EPS) * nw_ref[...] * _silu(hg_ref[:, sl])).astype(BF16)


def _hgrn(main, hg_lb, hg_norm_w):
    t = main.shape[0]
    c = HG_CHUNK
    w = HG_HEADS_PER_STEP * HG_DIM
    per = D_MODEL // w

    def col(b):
        return pl.BlockSpec((c, w), lambda g, i, b=b: (i, b * per + g))

    return pl.pallas_call(
        _hgrn_body,
        grid=(per, t // c),
        in_specs=[col(COL_HQ), col(COL_HF), col(COL_HI), col(COL_HG),
                  pl.BlockSpec((hg_lb.shape[0], w), lambda g, i: (0, g)),
                  pl.BlockSpec((1, HG_DIM), lambda g, i: (0, 0))],
        out_specs=pl.BlockSpec((c, w), lambda g, i: (i, g)),
        out_shape=jax.ShapeDtypeStruct((t, D_MODEL), BF16),
        scratch_shapes=[pltpu.VMEM((HG_HEADS_PER_STEP, HG_DIM, HG_DIM), F32)],
        compiler_params=_params(2),
        name="hgrn",
    )(main, main, main, main, hg_lb, hg_norm_w)


def _merge_body(ys_ref, yh_ref, ws_ref, wh_ref, gs_ref, gh_ref, o_ref):
    ps = jnp.dot(ys_ref[...], ws_ref[...], preferred_element_type=F32)
    ph = jnp.dot(yh_ref[...], wh_ref[...], preferred_element_type=F32)
    o_ref[...] = (_sigmoid(gs_ref[...]) * ps + _sigmoid(gh_ref[...]) * ph).astype(BF16)


def _merge(y_ssd, y_hg, w_s, w_h, main, tm=512, tn=1024):
    t = y_ssd.shape[0]
    tm = min(tm, t)
    per = D_MODEL // tn
    return pl.pallas_call(
        _merge_body,
        grid=(t // tm, D_MODEL // tn),
        in_specs=[pl.BlockSpec((tm, D_MODEL), lambda i, j: (i, 0)),
                  pl.BlockSpec((tm, D_MODEL), lambda i, j: (i, 0)),
                  pl.BlockSpec((D_MODEL, tn), lambda i, j: (0, j)),
                  pl.BlockSpec((D_MODEL, tn), lambda i, j: (0, j)),
                  pl.BlockSpec((tm, tn), lambda i, j: (i, COL_GS * per + j)),
                  pl.BlockSpec((tm, tn), lambda i, j: (i, COL_GH * per + j))],
        out_specs=pl.BlockSpec((tm, tn), lambda i, j: (i, j)),
        out_shape=jax.ShapeDtypeStruct((t, D_MODEL), BF16),
        compiler_params=_params(2),
        name="merge",
    )(y_ssd, y_hg, w_s, w_h, main, main)


def _outproj_body(mix_ref, w_ref, x_ref, nw_ref, x1_ref, h_ref, hb_ref):
    x1 = x_ref[...] + jnp.dot(mix_ref[...], w_ref[...], preferred_element_type=F32)
    x1_ref[...] = x1
    ms = jnp.mean(x1 * x1, axis=-1, keepdims=True)
    h = x1 * lax.rsqrt(ms + NORM_EPS) * nw_ref[...]
    h_ref[...] = h
    hb_ref[...] = h.astype(BF16)


def _outproj(mix, w_out, x, norm_w, tm=512):
    t = x.shape[0]
    tm = min(tm, t)
    row = pl.BlockSpec((tm, D_MODEL), lambda i: (i, 0))
    return pl.pallas_call(
        _outproj_body,
        grid=(t // tm,),
        in_specs=[row, pl.BlockSpec((D_MODEL, D_MODEL), lambda i: (0, 0)), row,
                  pl.BlockSpec((1, D_MODEL), lambda i: (0, 0))],
        out_specs=[row, row, row],
        out_shape=[jax.ShapeDtypeStruct((t, D_MODEL), F32),
                   jax.ShapeDtypeStruct((t, D_MODEL), F32),
                   jax.ShapeDtypeStruct((t, D_MODEL), BF16)],
        compiler_params=_params(1),
        name="outproj",
    )(mix, w_out, x, norm_w)


def _matmul_body(a_ref, b_ref, o_ref):
    o_ref[...] = jnp.dot(a_ref[...], b_ref[...], preferred_element_type=F32)


def _matmul(a, b, tm=512, tn=1024):
    m, k = a.shape
    n = b.shape[1]
    tm = min(tm, m)
    return pl.pallas_call(
        _matmul_body,
        grid=(m // tm, n // tn),
        in_specs=[pl.BlockSpec((tm, k), lambda i, j: (i, 0)),
                  pl.BlockSpec((k, tn), lambda i, j: (0, j))],
        out_specs=pl.BlockSpec((tm, tn), lambda i, j: (i, j)),
        out_shape=jax.ShapeDtypeStruct((m, n), F32),
        compiler_params=_params(2),
        name="peer_query",
    )(a, b)


def _topk_rows(s, k, on_pick):
    n = s.shape[0]
    rows = lax.broadcasted_iota(jnp.int32, s.shape, 0)
    for j in range(k):
        m = jnp.max(s, axis=0, keepdims=True)
        i = jnp.min(jnp.where(s == m, rows, n), axis=0, keepdims=True)
        hit = rows == i
        on_pick(j, m, i, hit)
        s = jnp.where(hit, -jnp.inf, s)


def _route_body(q_ref, keys_ref, idx_ref, gate_ref, s2_scr, i2_scr, ts_scr, cs_scr, ci_scr):
    k = PEER_TOPK
    q = q_ref[...]
    hi = lax.Precision.HIGHEST
    s_a = lax.dot_general(keys_ref[0, 0], q[:, :PEER_D_HALF], (((1,), (1,)), ((), ())),
                          precision=hi, preferred_element_type=F32)
    s_b = lax.dot_general(keys_ref[0, 1], q[:, PEER_D_HALF:], (((1,), (1,)), ((), ())),
                          precision=hi, preferred_element_type=F32)

    s1, i1 = [None] * k, [None] * k

    def pick_a(j, m, i, hit):
        s1[j], i1[j] = m, i

    def pick_b(j, m, i, hit):
        s2_scr[j:j + 1, :] = m
        i2_scr[j:j + 1, :] = i

    _topk_rows(s_a, k, pick_a)
    _topk_rows(s_b, k, pick_b)

    cs_scr[PEER_CAND_ROWS - SUBLANES:, :] = jnp.full((SUBLANES, cs_scr.shape[1]), -jnp.inf, F32)
    ci_scr[PEER_CAND_ROWS - SUBLANES:, :] = jnp.full((SUBLANES, ci_scr.shape[1]), -1, jnp.int32)
    off = 0
    for a in range(k):
        nb = k // (a + 1)
        cs_scr[off:off + nb, :] = s1[a] + s2_scr[0:nb, :]
        ci_scr[off:off + nb, :] = i1[a] * PEER_N_KEYS + i2_scr[0:nb, :]
        off += nb
    cand_i = ci_scr[...]

    def pick_c(j, m, i, hit):
        ts_scr[j:j + 1, :] = m
        idx_ref[j:j + 1, :] = jnp.max(jnp.where(hit, cand_i, -1), axis=0, keepdims=True)

    _topk_rows(cs_scr[...], k, pick_c)
    ts = ts_scr[...]
    e = jnp.exp(ts - ts[0:1, :])
    gate_ref[...] = e / jnp.sum(e, axis=0, keepdims=True)


def _route(q, sub_keys, tb=PEER_ROUTE_TOKENS):
    t = q.shape[0]
    tb = min(tb, t)
    k = PEER_TOPK
    out_spec = pl.BlockSpec((k, tb), lambda i, h: (h, i))
    return pl.pallas_call(
        _route_body,
        grid=(t // tb, PEER_HEADS),
        in_specs=[pl.BlockSpec((tb, 2 * PEER_D_HALF), lambda i, h: (i, h)),
                  pl.BlockSpec((1, 2, PEER_N_KEYS, PEER_D_HALF), lambda i, h: (h, 0, 0, 0))],
        out_specs=[out_spec, out_spec],
        out_shape=[jax.ShapeDtypeStruct((PEER_PAIRS, t), jnp.int32),
                   jax.ShapeDtypeStruct((PEER_PAIRS, t), F32)],
        scratch_shapes=[pltpu.VMEM((k, tb), F32), pltpu.VMEM((k, tb), jnp.int32),
                        pltpu.VMEM((k, tb), F32),
                        pltpu.VMEM((PEER_CAND_ROWS, tb), F32), pltpu.VMEM((PEER_CAND_ROWS, tb), jnp.int32)],
        compiler_params=_params(2),
        name="peer_route",
    )(q, sub_keys)


def _peer_body(idx_ref, idxn_ref, h_ref, x1_ref, gate_ref, fw_ref, tab_ref, o_ref, buf, acc_scr, sem):
    tb = h_ref.shape[0]
    step = pl.program_id(0)
    n_lane_tiles = D_MODEL // LANES
    rows_per_tile = PEER_PAIRS // (2 * n_lane_tiles)

    def start_rows(iref, tok, slot, r0, r1):
        for r in range(r0, r1):
            e = iref[tok, r]
            pltpu.make_async_copy(tab_ref.at[e], buf.at[slot, pl.ds(r, 1)],
                                  sem.at[slot]).start(priority=r % 2)

    def wait_token(slot):
        pltpu.make_async_copy(buf.at[slot], buf.at[slot], sem.at[slot]).wait()

    @pl.when(step == 0)
    def _():
        def first(p, carry):
            start_rows(idx_ref, p, p, 0, PEER_PAIRS)
            return carry
        lax.fori_loop(0, PEER_AHEAD, first, 0)

    eye = (lax.broadcasted_iota(jnp.int32, (PEER_PAIRS, PEER_PAIRS), 0)
           == lax.broadcasted_iota(jnp.int32, (PEER_PAIRS, PEER_PAIRS), 1))
    hi_mask = jnp.uint32(0xFFFF0000)
    sub_iota = lax.broadcasted_iota(jnp.int32, (SUBLANES, D_MODEL), 0)
    acc_scr[...] = jnp.zeros(acc_scr.shape, F32)

    def make_token(iref, ahead_row0):
        def token(tok, carry):
            slot = tok & (PEER_SLOTS - 1)
            nslot = (tok + PEER_AHEAD) & (PEER_SLOTS - 1)
            ntok = tok + ahead_row0
            wait_token(slot)
            hrow = h_ref[pl.ds(tok, 1), :]
            acc = jnp.zeros((PEER_PAIRS, LANES), F32)
            for j in range(n_lane_tiles):
                start_rows(iref, ntok, nslot, j * rows_per_tile, (j + 1) * rows_per_tile)
                w = buf[slot, :, j * LANES:(j + 1) * LANES]
                u = lax.bitcast_convert_type(w & hi_mask, F32)
                acc = acc + u * hrow[:, j * LANES:(j + 1) * LANES]
            act = jnp.sum(acc, axis=-1, keepdims=True)
            act = 0.5 * act * (1.0 + lax.erf(act * (2.0 ** -0.5)))
            grow = jnp.broadcast_to(gate_ref[pl.ds(tok, 1), :], (PEER_PAIRS, PEER_PAIRS))
            g = jnp.sum(jnp.where(eye, grow, 0.0), axis=-1, keepdims=True)
            wcol = jnp.broadcast_to(act * g, (PEER_PAIRS, LANES))
            acc8 = jnp.zeros((SUBLANES, D_MODEL), F32)
            for i in range(PEER_PAIRS // SUBLANES):
                r0 = PEER_PAIRS // 2 + i * rows_per_tile
                start_rows(iref, ntok, nslot, r0, r0 + rows_per_tile)
                rows = slice(i * SUBLANES, (i + 1) * SUBLANES)
                v = lax.bitcast_convert_type(buf[slot, rows, :] << 16, F32)
                acc8 = acc8 + v * jnp.concatenate([wcol[rows, :]] * n_lane_tiles, axis=1)
            orow = jnp.sum(acc8, axis=0, keepdims=True)
            base = pl.multiple_of((tok // SUBLANES) * SUBLANES, SUBLANES)
            tile = acc_scr[pl.ds(base, SUBLANES), :]
            acc_scr[pl.ds(base, SUBLANES), :] = jnp.where(sub_iota == tok % SUBLANES, orow, tile)
            return carry
        return token

    lax.fori_loop(0, tb - PEER_AHEAD, make_token(idx_ref, PEER_AHEAD), 0)
    lax.fori_loop(tb - PEER_AHEAD, tb, make_token(idxn_ref, PEER_AHEAD - tb), 0)

    @pl.when(step == pl.num_programs(0) - 1)
    def _():
        for p in range(PEER_AHEAD):
            wait_token(p)

    xo = x1_ref[...] + acc_scr[...]
    ms = jnp.mean(xo * xo, axis=-1, keepdims=True)
    o_ref[...] = xo * lax.rsqrt(ms + NORM_EPS) * fw_ref[...]


def _peer(idx, h, x1, gate, final_w, table, tb=PEER_TOKEN_BLOCK):
    t = h.shape[0]
    n_steps = t // tb
    assert tb % PEER_SLOTS == 0 and PEER_AHEAD < PEER_SLOTS and PEER_AHEAD <= tb
    row = pl.BlockSpec((tb, D_MODEL), lambda i: (i, 0))
    return pl.pallas_call(
        _peer_body,
        grid=(n_steps,),
        in_specs=[pl.BlockSpec((tb, PEER_PAIRS), lambda i: (i, 0), memory_space=pltpu.SMEM),
                  pl.BlockSpec((tb, PEER_PAIRS), lambda i: (jnp.minimum(i + 1, n_steps - 1), 0),
                               memory_space=pltpu.SMEM),
                  row, row,
                  pl.BlockSpec((tb, PEER_PAIRS), lambda i: (i, 0)),
                  pl.BlockSpec((1, D_MODEL), lambda i: (0, 0)),
                  pl.BlockSpec(memory_space=pl.ANY)],
        out_specs=row,
        out_shape=jax.ShapeDtypeStruct((t, D_MODEL), F32),
        scratch_shapes=[pltpu.VMEM((PEER_SLOTS, PEER_PAIRS, D_MODEL), jnp.uint32),
                        pltpu.VMEM((tb, D_MODEL), F32),
                        pltpu.SemaphoreType.DMA((PEER_SLOTS,))],
        compiler_params=_params(1),
        name="peer_ffn",
    )(idx, idx, h, x1, gate, final_w, table)


def _expand_heads(v, width):
    return jnp.repeat(v.astype(F32), width)[None, :]


def _layer(x, norm_mix_w, w_in, conv_w, conv_b, dt_bias, a_log, d_skip, ssd_norm_w, hg_lb, hg_norm_w,
           w_ssd_br, w_hg_br, w_out, norm_ffn_w, w_q, sub_keys, peer_u, peer_v, out_norm_w):
    d = D_MODEL
    conv_dim = conv_w.shape[1]
    o_dt = d + conv_dim
    w_dt = jnp.repeat(w_in[:, o_dt:o_dt + SSD_N_HEADS], SSD_HEAD_DIM, axis=1)
    w_main = jnp.concatenate([w_in[:, :o_dt], w_in[:, o_dt + SSD_N_HEADS:], w_dt], axis=1).astype(BF16)

    main = _inproj(x, norm_mix_w[None, :], w_main)
    y_ssd = _ssd(main, conv_w, conv_b[None, :], _expand_heads(dt_bias, SSD_HEAD_DIM),
                 _expand_heads(a_log, SSD_HEAD_DIM), _expand_heads(d_skip, SSD_HEAD_DIM),
                 ssd_norm_w[None, :])
    y_hg = _hgrn(main, hg_lb, hg_norm_w[None, :])
    mix = _merge(y_ssd, y_hg, w_ssd_br.astype(BF16), w_hg_br.astype(BF16), main)
    x1, h2, h2b = _outproj(mix, w_out.astype(BF16), x, norm_ffn_w[None, :])

    q = _matmul(h2b, w_q.astype(BF16))
    idx_t, gate_t = _route(q, sub_keys)
    ub = lax.bitcast_convert_type(peer_u.astype(BF16), jnp.uint16).astype(jnp.uint32)
    vb = lax.bitcast_convert_type(peer_v.astype(BF16), jnp.uint16).astype(jnp.uint32)
    table = ((ub << 16) | vb)[:, None, :]
    return _peer(idx_t.T, h2, x1, gate_t.T, out_norm_w[None, :], table)


def kernel(x, norm_mix_w, w_in, ssd_conv_w, ssd_conv_b, ssd_dt_bias, ssd_a_log, ssd_d, ssd_norm_w, hg_lb,
           hg_norm_w, w_ssd_br, w_hg_br, w_out, norm_ffn_w, peer_w_q, peer_sub_keys, peer_u, peer_v,
           final_norm_w):
    bsz, seq, d = x.shape
    depth = w_in.shape[0]
    assert bsz == 1 and depth == 1 and d == D_MODEL
    out = _layer(x.reshape(seq, d), norm_mix_w[0], w_in[0], ssd_conv_w[0], ssd_conv_b[0], ssd_dt_bias[0],
                 ssd_a_log[0], ssd_d[0], ssd_norm_w[0], hg_lb, hg_norm_w[0], w_ssd_br[0], w_hg_br[0],
                 w_out[0], norm_ffn_w[0], peer_w_q[0], peer_sub_keys[0], peer_u[0], peer_v[0], final_norm_w)
    return out.reshape(bsz, seq, d)
```

```python
import functools

import jax
import jax.numpy as jnp
from jax import lax
from jax.experimental import pallas as pl
from jax.experimental.pallas import tpu as pltpu

F32 = jnp.float32
BF16 = jnp.bfloat16

D_MODEL = 2048
NORM_EPS = 1e-6
LANES = 128
SUBLANES = 8
VMEM_LIMIT = 48 * 1024 * 1024

SSD_HEAD_DIM = 64
SSD_N_HEADS = 32
SSD_D_STATE = 128
SSD_N_GROUPS = 8
SSD_CONV_WIDTH = 4
SSD_CHUNK = 128
SSD_PAIRS = SSD_N_HEADS // 2
SSD_BC = SSD_N_GROUPS * SSD_D_STATE

HG_N_HEADS = 16
HG_DIM = 128
HG_CHUNK = 128
HG_HEADS_PER_STEP = 4
HG_LEVELS = (64, 32, 16, 8, 4, 2, 1)

PEER_HEADS = 8
PEER_N_KEYS = 128
PEER_TOPK = 16
PEER_D_HALF = 128
PEER_PAIRS = PEER_HEADS * PEER_TOPK
PEER_ROUTE_TOKENS = 512
PEER_CAND_ROWS = -(-sum(PEER_TOPK // (a + 1) for a in range(PEER_TOPK)) // SUBLANES) * SUBLANES
PEER_TOKEN_BLOCK = 64
PEER_SLOTS = 8
PEER_AHEAD = 6

COL_Z, COL_XS, COL_BC, COL_HQ, COL_HF, COL_HI, COL_HG, COL_GS, COL_GH, COL_DT = range(10)
N_COL_BLOCKS = 10


def _params(n_axes, vmem=VMEM_LIMIT, flags=None):
    return pltpu.CompilerParams(dimension_semantics=("arbitrary",) * n_axes,
                                vmem_limit_bytes=vmem, flags=flags)


def _sigmoid(v):
    return 0.5 * jnp.tanh(0.5 * v) + 0.5


def _silu(v):
    return v * _sigmoid(v)


def _split3(v):
    hi = v.astype(BF16)
    r1 = v - hi.astype(F32)
    mid = r1.astype(BF16)
    lo = (r1 - mid.astype(F32)).astype(BF16)
    return hi, mid, lo


def _tri_cumsum(tri_bf16, v):
    hi, mid, lo = _split3(v)
    dot = functools.partial(jnp.dot, preferred_element_type=F32)
    return dot(tri_bf16, hi) + dot(tri_bf16, mid) + dot(tri_bf16, lo)


def _dot_nt(a, b):
    return lax.dot_general(a, b, (((1,), (1,)), ((), ())), preferred_element_type=F32)


def _lower_tri(n):
    row = lax.broadcasted_iota(jnp.int32, (n, n), 0)
    col = lax.broadcasted_iota(jnp.int32, (n, n), 1)
    return row >= col


def _inproj_body(x_ref, nw_ref, w_ref, o_ref, h_scr):
    @pl.when(pl.program_id(1) == 0)
    def _():
        x = x_ref[...]
        ms = jnp.mean(x * x, axis=-1, keepdims=True)
        h_scr[...] = (x * lax.rsqrt(ms + NORM_EPS) * nw_ref[...]).astype(BF16)

    o_ref[...] = jnp.dot(h_scr[...], w_ref[...], preferred_element_type=F32)


def _inproj(x, norm_w, w, tm=1024, tn=1024):
    t, d = x.shape
    n = w.shape[1]
    tm = min(tm, t)
    return pl.pallas_call(
        _inproj_body,
        grid=(t // tm, n // tn),
        in_specs=[pl.BlockSpec((tm, d), lambda i, j: (i, 0)),
                  pl.BlockSpec((1, d), lambda i, j: (0, 0)),
                  pl.BlockSpec((d, tn), lambda i, j: (0, j))],
        out_specs=pl.BlockSpec((tm, tn), lambda i, j: (i, j)),
        out_shape=jax.ShapeDtypeStruct((t, n), F32),
        scratch_shapes=[pltpu.VMEM((tm, d), BF16)],
        compiler_params=_params(2),
        name="inproj",
    )(x, norm_w, w)


def _ssd_body(z_ref, xs_ref, bc_ref, dt_ref, cw_ref, cb_ref, dtb_ref, alog_ref, dsk_ref, nw_ref,
              y_ref, ext_scr, xc_scr, yb_scr, st_scr):
    c = SSD_CHUNK
    halo = SUBLANES

    @pl.when(pl.program_id(0) == 0)
    def _():
        ext_scr[0:halo, :] = jnp.zeros((halo, ext_scr.shape[1]), F32)
        st_scr[...] = jnp.zeros(st_scr.shape, F32)

    ext_scr[halo:halo + c, 0:D_MODEL] = xs_ref[...]
    ext_scr[halo:halo + c, D_MODEL:] = bc_ref[...]

    cblk = 512
    for lo in range(0, ext_scr.shape[1], cblk):
        acc = cb_ref[:, lo:lo + cblk] + cw_ref[SSD_CONV_WIDTH - 1:SSD_CONV_WIDTH, lo:lo + cblk] * \
            ext_scr[halo:halo + c, lo:lo + cblk]
        for j in range(1, SSD_CONV_WIDTH):
            k = SSD_CONV_WIDTH - 1 - j
            acc = acc + cw_ref[k:k + 1, lo:lo + cblk] * ext_scr[halo - j:halo - j + c, lo:lo + cblk]
        xc_scr[:, lo:lo + cblk] = _silu(acc)
    ext_scr[0:halo, :] = ext_scr[c:c + halo, :]

    tri = _lower_tri(c)
    tri_b = tri.astype(BF16)
    lane = lax.broadcasted_iota(jnp.int32, (c, LANES), 1)
    lo_half = lane < SSD_HEAD_DIM

    dt_raw = dt_ref[...] + dtb_ref[...]
    dt = jnp.maximum(dt_raw, 0.0) + jnp.log(1.0 + jnp.exp(-jnp.abs(dt_raw)))
    adt = -jnp.exp(alog_ref[...]) * dt
    acum = _tri_cumsum(tri_b, adt)

    for pr in range(SSD_PAIRS):
        g = pr // 2
        sl = slice(pr * LANES, (pr + 1) * LANES)
        xs_p = xc_scr[:, sl]
        xdt = xs_p * dt[:, sl]
        ac = acum[:, sl]
        ac_t = ac.T
        if pr % 2 == 0:
            bm = xc_scr[:, D_MODEL + g * SSD_D_STATE:D_MODEL + (g + 1) * SSD_D_STATE]
            cm = xc_scr[:, D_MODEL + SSD_BC + g * SSD_D_STATE:D_MODEL + SSD_BC + (g + 1) * SSD_D_STATE]
            cm_b = cm.astype(BF16)
            cb = _dot_nt(cm_b, bm.astype(BF16))
            bm_t = bm.T.astype(BF16)
        gs = []
        for hh in range(2):
            col = jnp.broadcast_to(ac[:, hh * SSD_HEAD_DIM:hh * SSD_HEAD_DIM + 1], (c, c))
            row = ac_t[hh * SSD_HEAD_DIM:hh * SSD_HEAD_DIM + 1, :]
            decay = jnp.exp(jnp.where(tri, col - row, -1e30))
            gs.append(cb * decay)
        gmat = jnp.concatenate(gs, axis=1).astype(BF16)
        x2 = jnp.concatenate([jnp.where(lo_half, xdt, 0.0), jnp.where(lo_half, 0.0, xdt)],
                             axis=0).astype(BF16)
        y_diag = jnp.dot(gmat, x2, preferred_element_type=F32)
        st = st_scr[pr]
        y_off = jnp.dot(cm_b, st.astype(BF16), preferred_element_type=F32) * jnp.exp(ac)
        a_end = ac[c - 1:c, :]
        wgt = (xdt * jnp.exp(a_end - ac)).astype(BF16)
        st_scr[pr] = st * jnp.exp(a_end) + jnp.dot(bm_t, wgt, preferred_element_type=F32)
        y_p = y_diag + y_off + dsk_ref[:, sl] * xs_p
        yb_scr[:, sl] = y_p * _silu(z_ref[:, sl])

    gw = D_MODEL // SSD_N_GROUPS
    for g in range(SSD_N_GROUPS):
        yg = yb_scr[:, g * gw:(g + 1) * gw]
        ms = jnp.mean(yg * yg, axis=-1, keepdims=True)
        y_ref[:, g * gw:(g + 1) * gw] = (yg * lax.rsqrt(ms + NORM_EPS) * nw_ref[:, g * gw:(g + 1) * gw]).astype(BF16)


def _ssd(main, conv_w, conv_b, dtb_x, alog_x, dskip_x, norm_w):
    t = main.shape[0]
    c = SSD_CHUNK
    conv_dim = conv_w.shape[1]

    def col(b):
        return pl.BlockSpec((c, D_MODEL), lambda i, b=b: (i, b))

    def full(shape):
        return pl.BlockSpec(shape, lambda i: (0,) * len(shape))

    return pl.pallas_call(
        _ssd_body,
        grid=(t // c,),
        in_specs=[col(COL_Z), col(COL_XS), col(COL_BC), col(COL_DT),
                  full((SSD_CONV_WIDTH, conv_dim)), full((1, conv_dim)),
                  full((1, D_MODEL)), full((1, D_MODEL)), full((1, D_MODEL)), full((1, D_MODEL))],
        out_specs=pl.BlockSpec((c, D_MODEL), lambda i: (i, 0)),
        out_shape=jax.ShapeDtypeStruct((t, D_MODEL), BF16),
        scratch_shapes=[pltpu.VMEM((c + 2 * SUBLANES, conv_dim), F32),
                        pltpu.VMEM((c, conv_dim), F32),
                        pltpu.VMEM((c, D_MODEL), F32),
                        pltpu.VMEM((SSD_PAIRS, SSD_D_STATE, LANES), F32)],
        compiler_params=_params(1),
        name="ssd",
    )(main, main, main, main, conv_w, conv_b, dtb_x, alog_x, dskip_x, norm_w)


def _hg_level_operand(q, k, b, blk, is_tgt):
    n = b.shape[0]
    if blk >= SUBLANES:
        pieces = []
        for m in range(n // (2 * blk)):
            lo, mid, hi = 2 * m * blk, (2 * m + 1) * blk, (2 * m + 2) * blk
            anchor = b[mid - 1:mid, :]
            pieces.append(k[lo:mid, :] * jnp.exp(anchor - b[lo:mid, :]))
            pieces.append(q[mid:hi, :] * jnp.exp(b[mid:hi, :] - anchor))
        return jnp.concatenate(pieces, axis=0)
    if blk == SUBLANES // 2:
        anchor = jnp.concatenate(
            [jnp.broadcast_to(b[t * SUBLANES + blk - 1:t * SUBLANES + blk, :], (SUBLANES, b.shape[1]))
             for t in range(n // SUBLANES)], axis=0)
    else:
        pos = lax.broadcasted_iota(jnp.int32, b.shape, 0) & (2 * blk - 1)
        anchor = b
        for j in range(2 * blk):
            shift = j - (blk - 1)
            if shift != 0:
                anchor = jnp.where(pos == j, pltpu.roll(b, shift % n, 0), anchor)
    d = b - anchor
    return jnp.where(is_tgt, q, k) * jnp.exp(jnp.where(is_tgt, d, -d))


def _hgrn_body(hq_ref, hf_ref, hi_ref, hg_ref, lb_ref, nw_ref, y_ref, st_scr):
    c = HG_CHUNK

    @pl.when(pl.program_id(1) == 0)
    def _():
        st_scr[...] = jnp.zeros(st_scr.shape, F32)

    tri_b = _lower_tri(c).astype(BF16)
    row = lax.broadcasted_iota(jnp.int32, (c, c), 0)
    col = lax.broadcasted_iota(jnp.int32, (c, c), 1)
    differ = row ^ col
    below = row > col
    level_masks = []
    for blk in HG_LEVELS:
        sh = blk.bit_length() - 1
        is_tgt = ((row >> sh) & 1) == 1 if blk < SUBLANES else None
        level_masks.append((is_tgt, ((differ >> sh) == 1) & below))

    lbr = lb_ref[...]
    e = jnp.exp(lbr - jnp.max(lbr, axis=0, keepdims=True))
    lb_all = e[0:1, :] / jnp.sum(e, axis=0, keepdims=True)

    for h in range(HG_HEADS_PER_STEP):
        sl = slice(h * HG_DIM, (h + 1) * HG_DIM)
        q = _silu(hq_ref[:, sl])
        lb = lb_all[:, sl]
        f = lb + (1.0 - lb) * _sigmoid(hf_ref[:, sl])
        k = 1.0 - f
        v = hi_ref[:, sl]
        v_b = v.astype(BF16)
        b = _tri_cumsum(tri_b, jnp.log(f))

        a = jnp.zeros((c, c), F32)
        for blk, (is_tgt, pair_mask) in zip(HG_LEVELS, level_masks):
            xl = _hg_level_operand(q, k, b, blk, is_tgt).astype(BF16)
            a = jnp.where(pair_mask, _dot_nt(xl, xl), a)
        o = jnp.dot(a.astype(BF16), v_b, preferred_element_type=F32)
        o = o + jnp.sum(q * k, axis=-1, keepdims=True) * v

        st = st_scr[h]
        o = o + _dot_nt((q * jnp.exp(b)).astype(BF16), st.astype(BF16))
        b_end = b[c - 1:c, :]
        ke = (k * jnp.exp(b_end - b)).astype(BF16)
        st_scr[h] = st * jnp.exp(b_end) + jnp.dot(v.T.astype(BF16), ke, preferred_element_type=F32)

        ms = jnp.mean(o * o, axis=-1, keepdims=True)
        y_ref[:, sl] = (o * lax.rsqrt(ms + NORM_EPS) * nw_ref[...] * _silu(hg_ref[:, sl])).astype(BF16)


def _hgrn(main, hg_lb, hg_norm_w):
    t = main.shape[0]
    c = HG_CHUNK
    w = HG_HEADS_PER_STEP * HG_DIM
    per = D_MODEL // w

    def col(b):
        return pl.BlockSpec((c, w), lambda g, i, b=b: (i, b * per + g))

    return pl.pallas_call(
        _hgrn_body,
        grid=(per, t // c),
        in_specs=[col(COL_HQ), col(COL_HF), col(COL_HI), col(COL_HG),
                  pl.BlockSpec((hg_lb.shape[0], w), lambda g, i: (0, g)),
                  pl.BlockSpec((1, HG_DIM), lambda g, i: (0, 0))],
        out_specs=pl.BlockSpec((c, w), lambda g, i: (i, g)),
        out_shape=jax.ShapeDtypeStruct((t, D_MODEL), BF16),
        scratch_shapes=[pltpu.VMEM((HG_HEADS_PER_STEP, HG_DIM, HG_DIM), F32)],
        compiler_params=_params(2),
        name="hgrn",
    )(main, main, main, main, hg_lb, hg_norm_w)


def _merge_body(ys_ref, yh_ref, ws_ref, wh_ref, gs_ref, gh_ref, o_ref):
    ps = jnp.dot(ys_ref[...], ws_ref[...], preferred_element_type=F32)
    ph = jnp.dot(yh_ref[...], wh_ref[...], preferred_element_type=F32)
    o_ref[...] = (_sigmoid(gs_ref[...]) * ps + _sigmoid(gh_ref[...]) * ph).astype(BF16)


def _merge(y_ssd, y_hg, w_s, w_h, main, tm=512, tn=1024):
    t = y_ssd.shape[0]
    tm = min(tm, t)
    per = D_MODEL // tn
    return pl.pallas_call(
        _merge_body,
        grid=(t // tm, D_MODEL // tn),
        in_specs=[pl.BlockSpec((tm, D_MODEL), lambda i, j: (i, 0)),
                  pl.BlockSpec((tm, D_MODEL), lambda i, j: (i, 0)),
                  pl.BlockSpec((D_MODEL, tn), lambda i, j: (0, j)),
                  pl.BlockSpec((D_MODEL, tn), lambda i, j: (0, j)),
                  pl.BlockSpec((tm, tn), lambda i, j: (i, COL_GS * per + j)),
                  pl.BlockSpec((tm, tn), lambda i, j: (i, COL_GH * per + j))],
        out_specs=pl.BlockSpec((tm, tn), lambda i, j: (i, j)),
        out_shape=jax.ShapeDtypeStruct((t, D_MODEL), BF16),
        compiler_params=_params(2),
        name="merge",
    )(y_ssd, y_hg, w_s, w_h, main, main)


def _outproj_body(mix_ref, w_ref, x_ref, nw_ref, x1_ref, h_ref, hb_ref):
    x1 = x_ref[...] + jnp.dot(mix_ref[...], w_ref[...], preferred_element_type=F32)
    x1_ref[...] = x1
    ms = jnp.mean(x1 * x1, axis=-1, keepdims=True)
    h = x1 * lax.rsqrt(ms + NORM_EPS) * nw_ref[...]
    h_ref[...] = h
    hb_ref[...] = h.astype(BF16)


def _outproj(mix, w_out, x, norm_w, tm=512):
    t = x.shape[0]
    tm = min(tm, t)
    row = pl.BlockSpec((tm, D_MODEL), lambda i: (i, 0))
    return pl.pallas_call(
        _outproj_body,
        grid=(t // tm,),
        in_specs=[row, pl.BlockSpec((D_MODEL, D_MODEL), lambda i: (0, 0)), row,
                  pl.BlockSpec((1, D_MODEL), lambda i: (0, 0))],
        out_specs=[row, row, row],
        out_shape=[jax.ShapeDtypeStruct((t, D_MODEL), F32),
                   jax.ShapeDtypeStruct((t, D_MODEL), F32),
                   jax.ShapeDtypeStruct((t, D_MODEL), BF16)],
        compiler_params=_params(1),
        name="outproj",
    )(mix, w_out, x, norm_w)


def _matmul_body(a_ref, b_ref, o_ref):
    o_ref[...] = jnp.dot(a_ref[...], b_ref[...], preferred_element_type=F32)


def _matmul(a, b, tm=512, tn=1024):
    m, k = a.shape
    n = b.shape[1]
    tm = min(tm, m)
    return pl.pallas_call(
        _matmul_body,
        grid=(m // tm, n // tn),
        in_specs=[pl.BlockSpec((tm, k), lambda i, j: (i, 0)),
                  pl.BlockSpec((k, tn), lambda i, j: (0, j))],
        out_specs=pl.BlockSpec((tm, tn), lambda i, j: (i, j)),
        out_shape=jax.ShapeDtypeStruct((m, n), F32),
        compiler_params=_params(2),
        name="peer_query",
    )(a, b)


def _topk_rows(s, k, on_pick):
    n = s.shape[0]
    rows = lax.broadcasted_iota(jnp.int32, s.shape, 0)
    for j in range(k):
        m = jnp.max(s, axis=0, keepdims=True)
        i = jnp.min(jnp.where(s == m, rows, n), axis=0, keepdims=True)
        hit = rows == i
        on_pick(j, m, i, hit)
        s = jnp.where(hit, -jnp.inf, s)


def _route_body(q_ref, keys_ref, idx_ref, gate_ref, s2_scr, i2_scr, ts_scr, cs_scr, ci_scr):
    k = PEER_TOPK
    q = q_ref[...]
    hi = lax.Precision.HIGHEST
    s_a = lax.dot_general(keys_ref[0, 0], q[:, :PEER_D_HALF], (((1,), (1,)), ((), ())),
                          precision=hi, preferred_element_type=F32)
    s_b = lax.dot_general(keys_ref[0, 1], q[:, PEER_D_HALF:], (((1,), (1,)), ((), ())),
                          precision=hi, preferred_element_type=F32)

    s1, i1 = [None] * k, [None] * k

    def pick_a(j, m, i, hit):
        s1[j], i1[j] = m, i

    def pick_b(j, m, i, hit):
        s2_scr[j:j + 1, :] = m
        i2_scr[j:j + 1, :] = i

    _topk_rows(s_a, k, pick_a)
    _topk_rows(s_b, k, pick_b)

    cs_scr[PEER_CAND_ROWS - SUBLANES:, :] = jnp.full((SUBLANES, cs_scr.shape[1]), -jnp.inf, F32)
    ci_scr[PEER_CAND_ROWS - SUBLANES:, :] = jnp.full((SUBLANES, ci_scr.shape[1]), -1, jnp.int32)
    off = 0
    for a in range(k):
        nb = k // (a + 1)
        cs_scr[off:off + nb, :] = s1[a] + s2_scr[0:nb, :]
        ci_scr[off:off + nb, :] = i1[a] * PEER_N_KEYS + i2_scr[0:nb, :]
        off += nb
    cand_i = ci_scr[...]

    def pick_c(j, m, i, hit):
        ts_scr[j:j + 1, :] = m
        idx_ref[j:j + 1, :] = jnp.max(jnp.where(hit, cand_i, -1), axis=0, keepdims=True)

    _topk_rows(cs_scr[...], k, pick_c)
    ts = ts_scr[...]
    e = jnp.exp(ts - ts[0:1, :])
    gate_ref[...] = e / jnp.sum(e, axis=0, keepdims=True)


def _route(q, sub_keys, tb=PEER_ROUTE_TOKENS):
    t = q.shape[0]
    tb = min(tb, t)
    k = PEER_TOPK
    out_spec = pl.BlockSpec((k, tb), lambda i, h: (h, i))
    return pl.pallas_call(
        _route_body,
        grid=(t // tb, PEER_HEADS),
        in_specs=[pl.BlockSpec((tb, 2 * PEER_D_HALF), lambda i, h: (i, h)),
                  pl.BlockSpec((1, 2, PEER_N_KEYS, PEER_D_HALF), lambda i, h: (h, 0, 0, 0))],
        out_specs=[out_spec, out_spec],
        out_shape=[jax.ShapeDtypeStruct((PEER_PAIRS, t), jnp.int32),
                   jax.ShapeDtypeStruct((PEER_PAIRS, t), F32)],
        scratch_shapes=[pltpu.VMEM((k, tb), F32), pltpu.VMEM((k, tb), jnp.int32),
                        pltpu.VMEM((k, tb), F32),
                        pltpu.VMEM((PEER_CAND_ROWS, tb), F32), pltpu.VMEM((PEER_CAND_ROWS, tb), jnp.int32)],
        compiler_params=_params(2),
        name="peer_route",
    )(q, sub_keys)


def _peer_body(idx_ref, h_ref, x1_ref, gate_ref, fw_ref, tab_ref, o_ref, buf, acc_scr, uacc_scr, vacc_scr,
               wc_scr, sem):
    tb = h_ref.shape[0]
    step = pl.program_id(0)
    n_lane_tiles = D_MODEL // LANES
    half = PEER_PAIRS // 2
    n_part = half // n_lane_tiles
    part = PEER_PAIRS // n_part
    width = D_MODEL // n_part

    def start_row(tok, slot, r):
        e = idx_ref[tok, r]
        pltpu.make_async_copy(tab_ref.at[e], buf.at[slot, pl.ds(r, 1)], sem.at[slot]).start(priority=r % 2)

    def wait_token(slot):
        pltpu.make_async_copy(buf.at[slot], buf.at[slot], sem.at[slot]).wait()

    @pl.when(step == 0)
    def _():
        def first(p, carry):
            for r in range(PEER_PAIRS):
                start_row(p, p, r)
            return carry
        lax.fori_loop(0, PEER_AHEAD, first, 0)

    eye = (lax.broadcasted_iota(jnp.int32, (PEER_PAIRS, PEER_PAIRS), 0)
           == lax.broadcasted_iota(jnp.int32, (PEER_PAIRS, PEER_PAIRS), 1))
    hi_mask = jnp.uint32(0xFFFF0000)
    sub_iota = lax.broadcasted_iota(jnp.int32, (SUBLANES, D_MODEL), 0)
    acc_scr[...] = jnp.zeros(acc_scr.shape, F32)

    def u_pass(tok):
        slot = tok & (PEER_SLOTS - 1)
        nslot = (tok + PEER_AHEAD) & (PEER_SLOTS - 1)
        wait_token(slot)
        hrow = h_ref[pl.ds(tok, 1), :]
        for j in range(n_lane_tiles):
            hj = hrow[:, j * LANES:(j + 1) * LANES]
            for p in range(n_part):
                start_row(tok + PEER_AHEAD, nslot, j * n_part + p)
                rows = slice(p * part, (p + 1) * part)
                prod = lax.bitcast_convert_type(buf[slot, rows, j * LANES:(j + 1) * LANES] & hi_mask, F32) * hj
                uacc_scr[rows, :] = prod if j == 0 else uacc_scr[rows, :] + prod
        act = jnp.sum(uacc_scr[...], axis=-1, keepdims=True)
        act = 0.5 * act * (1.0 + lax.erf(act * (2.0 ** -0.5)))
        grow = jnp.broadcast_to(gate_ref[pl.ds(tok, 1), :], (PEER_PAIRS, PEER_PAIRS))
        g = jnp.sum(jnp.where(eye, grow, 0.0), axis=-1, keepdims=True)
        return jnp.broadcast_to(act * g, (PEER_PAIRS, LANES))

    def v_pass(tok, wcol):
        slot = tok & (PEER_SLOTS - 1)
        nslot = (tok + PEER_AHEAD) & (PEER_SLOTS - 1)
        for i in range(PEER_PAIRS // SUBLANES):
            rows = slice(i * SUBLANES, (i + 1) * SUBLANES)
            wblk = jnp.concatenate([wcol[rows, :]] * (width // LANES), axis=1)
            for p in range(n_part):
                start_row(tok + PEER_AHEAD, nslot, half + i * n_part + p)
                cols = slice(p * width, (p + 1) * width)
                prod = lax.bitcast_convert_type(buf[slot, rows, cols] << 16, F32) * wblk
                vacc_scr[:, cols] = prod if i == 0 else vacc_scr[:, cols] + prod
        orow = jnp.sum(vacc_scr[...], axis=0, keepdims=True)
        base = (tok // SUBLANES) * SUBLANES
        if not isinstance(base, int):
            base = pl.multiple_of(base, SUBLANES)
        tile = acc_scr[pl.ds(base, SUBLANES), :]
        acc_scr[pl.ds(base, SUBLANES), :] = jnp.where(sub_iota == tok % SUBLANES, orow, tile)

    wc_scr[...] = u_pass(0)

    def token(tok, carry):
        w_prev = wc_scr[...]
        w_cur = u_pass(tok)
        v_pass(tok - 1, w_prev)
        wc_scr[...] = w_cur
        return carry

    lax.fori_loop(1, tb, token, 0)
    v_pass(tb - 1, wc_scr[...])

    @pl.when(step == pl.num_programs(0) - 1)
    def _():
        for p in range(PEER_AHEAD):
            wait_token(p)

    xo = x1_ref[...] + acc_scr[...]
    ms = jnp.mean(xo * xo, axis=-1, keepdims=True)
    o_ref[...] = xo * lax.rsqrt(ms + NORM_EPS) * fw_ref[...]


def _peer(idx, h, x1, gate, final_w, table, tb=PEER_TOKEN_BLOCK):
    t = h.shape[0]
    n_steps = t // tb
    assert tb % PEER_SLOTS == 0 and 2 <= PEER_AHEAD < PEER_SLOTS - 1 and PEER_AHEAD <= tb
    blocks = idx.reshape(n_steps, tb, PEER_PAIRS)
    heads = jnp.roll(blocks[:, :PEER_SLOTS], -1, axis=0)
    idx = jnp.concatenate([blocks, heads], axis=1).reshape(n_steps * (tb + PEER_SLOTS), PEER_PAIRS)
    row = pl.BlockSpec((tb, D_MODEL), lambda i: (i, 0))
    return pl.pallas_call(
        _peer_body,
        grid=(n_steps,),
        in_specs=[pl.BlockSpec((tb + PEER_SLOTS, PEER_PAIRS), lambda i: (i, 0), memory_space=pltpu.SMEM),
                  row, row,
                  pl.BlockSpec((tb, PEER_PAIRS), lambda i: (i, 0)),
                  pl.BlockSpec((1, D_MODEL), lambda i: (0, 0)),
                  pl.BlockSpec(memory_space=pl.ANY)],
        out_specs=row,
        out_shape=jax.ShapeDtypeStruct((t, D_MODEL), F32),
        scratch_shapes=[pltpu.VMEM((PEER_SLOTS, PEER_PAIRS, D_MODEL), jnp.uint32),
                        pltpu.VMEM((tb, D_MODEL), F32),
                        pltpu.VMEM((PEER_PAIRS, LANES), F32),
                        pltpu.VMEM((SUBLANES, D_MODEL), F32),
                        pltpu.VMEM((PEER_PAIRS, LANES), F32),
                        pltpu.SemaphoreType.DMA((PEER_SLOTS,))],
        compiler_params=_params(1),
        name="peer_ffn",
    )(idx, h, x1, gate, final_w, table)


def _expand_heads(v, width):
    return jnp.repeat(v.astype(F32), width)[None, :]


def _layer(x, norm_mix_w, w_in, conv_w, conv_b, dt_bias, a_log, d_skip, ssd_norm_w, hg_lb, hg_norm_w,
           w_ssd_br, w_hg_br, w_out, norm_ffn_w, w_q, sub_keys, peer_u, peer_v, out_norm_w):
    d = D_MODEL
    conv_dim = conv_w.shape[1]
    o_dt = d + conv_dim
    w_dt = jnp.repeat(w_in[:, o_dt:o_dt + SSD_N_HEADS], SSD_HEAD_DIM, axis=1)
    w_main = jnp.concatenate([w_in[:, :o_dt], w_in[:, o_dt + SSD_N_HEADS:], w_dt], axis=1).astype(BF16)

    main = _inproj(x, norm_mix_w[None, :], w_main)
    y_ssd = _ssd(main, conv_w, conv_b[None, :], _expand_heads(dt_bias, SSD_HEAD_DIM),
                 _expand_heads(a_log, SSD_HEAD_DIM), _expand_heads(d_skip, SSD_HEAD_DIM),
                 ssd_norm_w[None, :])
    y_hg = _hgrn(main, hg_lb, hg_norm_w[None, :])
    mix = _merge(y_ssd, y_hg, w_ssd_br.astype(BF16), w_hg_br.astype(BF16), main)
    x1, h2, h2b = _outproj(mix, w_out.astype(BF16), x, norm_ffn_w[None, :])

    q = _matmul(h2b, w_q.astype(BF16))
    idx_t, gate_t = _route(q, sub_keys)
    ub = lax.bitcast_convert_type(peer_u.astype(BF16), jnp.uint16).astype(jnp.uint32)
    vb = lax.bitcast_convert_type(peer_v.astype(BF16), jnp.uint16).astype(jnp.uint32)
    table = ((ub << 16) | vb)[:, None, :]
    return _peer(idx_t.T, h2, x1, gate_t.T, out_norm_w[None, :], table)


def kernel(x, norm_mix_w, w_in, ssd_conv_w, ssd_conv_b, ssd_dt_bias, ssd_a_log, ssd_d, ssd_norm_w, hg_lb,
           hg_norm_w, w_ssd_br, w_hg_br, w_out, norm_ffn_w, peer_w_q, peer_sub_keys, peer_u, peer_v,
           final_norm_w):
    bsz, seq, d = x.shape
    depth = w_in.shape[0]
    assert bsz == 1 and depth == 1 and d == D_MODEL
    out = _layer(x.reshape(seq, d), norm_mix_w[0], w_in[0], ssd_conv_w[0], ssd_conv_b[0], ssd_dt_bias[0],
                 ssd_a_log[0], ssd_d[0], ssd_norm_w[0], hg_lb, hg_norm_w[0], w_ssd_br[0], w_hg_br[0],
                 w_out[0], norm_ffn_w[0], peer_w_q[0], peer_sub_keys[0], peer_u[0], peer_v[0], final_norm_w)
    return out.reshape(bsz, seq, d)
```

```python
import functools

import jax
import jax.numpy as jnp
from jax import lax
from jax.experimental import pallas as pl
from jax.experimental.pallas import tpu as pltpu

F32 = jnp.float32
BF16 = jnp.bfloat16

D_MODEL = 2048
NORM_EPS = 1e-6
LANES = 128
SUBLANES = 8
VMEM_LIMIT = 48 * 1024 * 1024

SSD_HEAD_DIM = 64
SSD_N_HEADS = 32
SSD_D_STATE = 128
SSD_N_GROUPS = 8
SSD_CONV_WIDTH = 4
SSD_CHUNK = 128
SSD_PAIRS = SSD_N_HEADS // 2
SSD_BC = SSD_N_GROUPS * SSD_D_STATE

HG_N_HEADS = 16
HG_DIM = 128
HG_CHUNK = 128
HG_HEADS_PER_STEP = 16
HG_LEVELS = (64, 32, 16, 8, 4, 2, 1)

PEER_HEADS = 8
PEER_N_KEYS = 128
PEER_TOPK = 16
PEER_D_HALF = 128
PEER_PAIRS = PEER_HEADS * PEER_TOPK
PEER_ROUTE_TOKENS = 512
PEER_CAND_ROWS = -(-sum(PEER_TOPK // (a + 1) for a in range(PEER_TOPK)) // SUBLANES) * SUBLANES
PEER_TOKEN_BLOCK = 64
PEER_SLOTS = 8
PEER_AHEAD = 6

COL_Z, COL_XS, COL_BC, COL_HQ, COL_HF, COL_HI, COL_HG, COL_GS, COL_GH, COL_DT = range(10)
N_COL_BLOCKS = 10


def _params(n_axes, vmem=VMEM_LIMIT, flags=None):
    return pltpu.CompilerParams(dimension_semantics=("arbitrary",) * n_axes,
                                vmem_limit_bytes=vmem, flags=flags)


def _sigmoid(v):
    return 0.5 * jnp.tanh(0.5 * v) + 0.5


def _silu(v):
    return v * _sigmoid(v)


def _split3(v):
    hi = v.astype(BF16)
    r1 = v - hi.astype(F32)
    mid = r1.astype(BF16)
    lo = (r1 - mid.astype(F32)).astype(BF16)
    return hi, mid, lo


def _tri_cumsum(tri_bf16, v):
    hi, mid, lo = _split3(v)
    dot = functools.partial(jnp.dot, preferred_element_type=F32)
    return dot(tri_bf16, hi) + dot(tri_bf16, mid) + dot(tri_bf16, lo)


def _dot_nt(a, b):
    return lax.dot_general(a, b, (((1,), (1,)), ((), ())), preferred_element_type=F32)


def _lower_tri(n):
    row = lax.broadcasted_iota(jnp.int32, (n, n), 0)
    col = lax.broadcasted_iota(jnp.int32, (n, n), 1)
    return row >= col


def _inproj_body(x_ref, nw_ref, w_ref, o_ref, h_scr):
    @pl.when(pl.program_id(1) == 0)
    def _():
        x = x_ref[...]
        ms = jnp.mean(x * x, axis=-1, keepdims=True)
        h_scr[...] = (x * lax.rsqrt(ms + NORM_EPS) * nw_ref[...]).astype(BF16)

    o_ref[...] = jnp.dot(h_scr[...], w_ref[...], preferred_element_type=F32)


def _inproj(x, norm_w, w, tm=1024, tn=1024):
    t, d = x.shape
    n = w.shape[1]
    tm = min(tm, t)
    return pl.pallas_call(
        _inproj_body,
        grid=(t // tm, n // tn),
        in_specs=[pl.BlockSpec((tm, d), lambda i, j: (i, 0)),
                  pl.BlockSpec((1, d), lambda i, j: (0, 0)),
                  pl.BlockSpec((d, tn), lambda i, j: (0, j))],
        out_specs=pl.BlockSpec((tm, tn), lambda i, j: (i, j)),
        out_shape=jax.ShapeDtypeStruct((t, n), F32),
        scratch_shapes=[pltpu.VMEM((tm, d), BF16)],
        compiler_params=_params(2),
        name="inproj",
    )(x, norm_w, w)


def _ssd_body(z_ref, xs_ref, bc_ref, dt_ref, cw_ref, cb_ref, dtb_ref, alog_ref, dsk_ref, nw_ref,
              y_ref, ext_scr, xc_scr, yb_scr, st_scr):
    c = SSD_CHUNK
    halo = SUBLANES

    @pl.when(pl.program_id(0) == 0)
    def _():
        ext_scr[0:halo, :] = jnp.zeros((halo, ext_scr.shape[1]), F32)
        st_scr[...] = jnp.zeros(st_scr.shape, F32)

    ext_scr[halo:halo + c, 0:D_MODEL] = xs_ref[...]
    ext_scr[halo:halo + c, D_MODEL:] = bc_ref[...]

    cblk = 512
    for lo in range(0, ext_scr.shape[1], cblk):
        acc = cb_ref[:, lo:lo + cblk] + cw_ref[SSD_CONV_WIDTH - 1:SSD_CONV_WIDTH, lo:lo + cblk] * \
            ext_scr[halo:halo + c, lo:lo + cblk]
        for j in range(1, SSD_CONV_WIDTH):
            k = SSD_CONV_WIDTH - 1 - j
            acc = acc + cw_ref[k:k + 1, lo:lo + cblk] * ext_scr[halo - j:halo - j + c, lo:lo + cblk]
        xc_scr[:, lo:lo + cblk] = _silu(acc)
    ext_scr[0:halo, :] = ext_scr[c:c + halo, :]

    tri = _lower_tri(c)
    tri_b = tri.astype(BF16)
    lane = lax.broadcasted_iota(jnp.int32, (c, LANES), 1)
    lo_half = lane < SSD_HEAD_DIM

    dt_raw = dt_ref[...] + dtb_ref[...]
    dt = jnp.maximum(dt_raw, 0.0) + jnp.log(1.0 + jnp.exp(-jnp.abs(dt_raw)))
    adt = -jnp.exp(alog_ref[...]) * dt
    acum = _tri_cumsum(tri_b, adt)

    for pr in range(SSD_PAIRS):
        g = pr // 2
        sl = slice(pr * LANES, (pr + 1) * LANES)
        xs_p = xc_scr[:, sl]
        xdt = xs_p * dt[:, sl]
        ac = acum[:, sl]
        ac_t = ac.T
        if pr % 2 == 0:
            bm = xc_scr[:, D_MODEL + g * SSD_D_STATE:D_MODEL + (g + 1) * SSD_D_STATE]
            cm = xc_scr[:, D_MODEL + SSD_BC + g * SSD_D_STATE:D_MODEL + SSD_BC + (g + 1) * SSD_D_STATE]
            cm_b = cm.astype(BF16)
            cb = _dot_nt(cm_b, bm.astype(BF16))
            bm_t = bm.T.astype(BF16)
        gs = []
        for hh in range(2):
            col = jnp.broadcast_to(ac[:, hh * SSD_HEAD_DIM:hh * SSD_HEAD_DIM + 1], (c, c))
            row = ac_t[hh * SSD_HEAD_DIM:hh * SSD_HEAD_DIM + 1, :]
            decay = jnp.exp(jnp.where(tri, col - row, -1e30))
            gs.append(cb * decay)
        gmat = jnp.concatenate(gs, axis=1).astype(BF16)
        x2 = jnp.concatenate([jnp.where(lo_half, xdt, 0.0), jnp.where(lo_half, 0.0, xdt)],
                             axis=0).astype(BF16)
        y_diag = jnp.dot(gmat, x2, preferred_element_type=F32)
        st = st_scr[pr]
        y_off = jnp.dot(cm_b, st.astype(BF16), preferred_element_type=F32) * jnp.exp(ac)
        a_end = ac[c - 1:c, :]
        wgt = (xdt * jnp.exp(a_end - ac)).astype(BF16)
        st_scr[pr] = st * jnp.exp(a_end) + jnp.dot(bm_t, wgt, preferred_element_type=F32)
        y_p = y_diag + y_off + dsk_ref[:, sl] * xs_p
        yb_scr[:, sl] = y_p * _silu(z_ref[:, sl])

    gw = D_MODEL // SSD_N_GROUPS
    for g in range(SSD_N_GROUPS):
        yg = yb_scr[:, g * gw:(g + 1) * gw]
        ms = jnp.mean(yg * yg, axis=-1, keepdims=True)
        y_ref[:, g * gw:(g + 1) * gw] = (yg * lax.rsqrt(ms + NORM_EPS) * nw_ref[:, g * gw:(g + 1) * gw]).astype(BF16)


def _ssd(main, conv_w, conv_b, dtb_x, alog_x, dskip_x, norm_w):
    t = main.shape[0]
    c = SSD_CHUNK
    conv_dim = conv_w.shape[1]

    def col(b):
        return pl.BlockSpec((c, D_MODEL), lambda i, b=b: (i, b))

    def full(shape):
        return pl.BlockSpec(shape, lambda i: (0,) * len(shape))

    return pl.pallas_call(
        _ssd_body,
        grid=(t // c,),
        in_specs=[col(COL_Z), col(COL_XS), col(COL_BC), col(COL_DT),
                  full((SSD_CONV_WIDTH, conv_dim)), full((1, conv_dim)),
                  full((1, D_MODEL)), full((1, D_MODEL)), full((1, D_MODEL)), full((1, D_MODEL))],
        out_specs=pl.BlockSpec((c, D_MODEL), lambda i: (i, 0)),
        out_shape=jax.ShapeDtypeStruct((t, D_MODEL), BF16),
        scratch_shapes=[pltpu.VMEM((c + 2 * SUBLANES, conv_dim), F32),
                        pltpu.VMEM((c, conv_dim), F32),
                        pltpu.VMEM((c, D_MODEL), F32),
                        pltpu.VMEM((SSD_PAIRS, SSD_D_STATE, LANES), F32)],
        compiler_params=_params(1),
        name="ssd",
    )(main, main, main, main, conv_w, conv_b, dtb_x, alog_x, dskip_x, norm_w)


def _hg_level_operand(q, k, b, blk, is_tgt):
    n = b.shape[0]
    if blk >= SUBLANES:
        pieces = []
        for m in range(n // (2 * blk)):
            lo, mid, hi = 2 * m * blk, (2 * m + 1) * blk, (2 * m + 2) * blk
            anchor = b[mid - 1:mid, :]
            pieces.append(k[lo:mid, :] * jnp.exp(anchor - b[lo:mid, :]))
            pieces.append(q[mid:hi, :] * jnp.exp(b[mid:hi, :] - anchor))
        return jnp.concatenate(pieces, axis=0)
    if blk == SUBLANES // 2:
        anchor = jnp.concatenate(
            [jnp.broadcast_to(b[t * SUBLANES + blk - 1:t * SUBLANES + blk, :], (SUBLANES, b.shape[1]))
             for t in range(n // SUBLANES)], axis=0)
    else:
        pos = lax.broadcasted_iota(jnp.int32, b.shape, 0) & (2 * blk - 1)
        anchor = b
        for j in range(2 * blk):
            shift = j - (blk - 1)
            if shift != 0:
                anchor = jnp.where(pos == j, pltpu.roll(b, shift % n, 0), anchor)
    d = b - anchor
    return jnp.where(is_tgt, q, k) * jnp.exp(jnp.where(is_tgt, d, -d))


def _hgrn_body(hq_ref, hf_ref, hi_ref, hg_ref, lb_ref, nw_ref, y_ref, st_scr):
    c = HG_CHUNK

    @pl.when(pl.program_id(1) == 0)
    def _():
        st_scr[...] = jnp.zeros(st_scr.shape, F32)

    tri_b = _lower_tri(c).astype(BF16)
    row = lax.broadcasted_iota(jnp.int32, (c, c), 0)
    col = lax.broadcasted_iota(jnp.int32, (c, c), 1)
    differ = row ^ col
    below = row > col
    level_masks = []
    for blk in HG_LEVELS:
        sh = blk.bit_length() - 1
        is_tgt = ((row >> sh) & 1) == 1 if blk < SUBLANES else None
        level_masks.append((is_tgt, ((differ >> sh) == 1) & below))

    lbr = lb_ref[...]
    e = jnp.exp(lbr - jnp.max(lbr, axis=0, keepdims=True))
    lb_all = e[0:1, :] / jnp.sum(e, axis=0, keepdims=True)

    for h in range(HG_HEADS_PER_STEP):
        sl = slice(h * HG_DIM, (h + 1) * HG_DIM)
        q = _silu(hq_ref[:, sl])
        lb = lb_all[:, sl]
        f = lb + (1.0 - lb) * _sigmoid(hf_ref[:, sl])
        k = 1.0 - f
        v = hi_ref[:, sl]
        v_b = v.astype(BF16)
        b = _tri_cumsum(tri_b, jnp.log(f))

        a = jnp.zeros((c, c), F32)
        for blk, (is_tgt, pair_mask) in zip(HG_LEVELS, level_masks):
            xl = _hg_level_operand(q, k, b, blk, is_tgt).astype(BF16)
            a = jnp.where(pair_mask, _dot_nt(xl, xl), a)
        o = jnp.dot(a.astype(BF16), v_b, preferred_element_type=F32)
        o = o + jnp.sum(q * k, axis=-1, keepdims=True) * v

        st = st_scr[h]
        o = o + _dot_nt((q * jnp.exp(b)).astype(BF16), st.astype(BF16))
        b_end = b[c - 1:c, :]
        ke = (k * jnp.exp(b_end - b)).astype(BF16)
        st_scr[h] = st * jnp.exp(b_end) + jnp.dot(v.T.astype(BF16), ke, preferred_element_type=F32)

        ms = jnp.mean(o * o, axis=-1, keepdims=True)
        y_ref[:, sl] = (o * lax.rsqrt(ms + NORM_EPS) * nw_ref[...] * _silu(hg_ref[:, sl])).astype(BF16)


def _hgrn(main, hg_lb, hg_norm_w):
    t = main.shape[0]
    c = HG_CHUNK
    w = HG_HEADS_PER_STEP * HG_DIM
    per = D_MODEL // w

    def col(b):
        return pl.BlockSpec((c, w), lambda g, i, b=b: (i, b * per + g))

    return pl.pallas_call(
        _hgrn_body,
        grid=(per, t // c),
        in_specs=[col(COL_HQ), col(COL_HF), col(COL_HI), col(COL_HG),
                  pl.BlockSpec((hg_lb.shape[0], w), lambda g, i: (0, g)),
                  pl.BlockSpec((1, HG_DIM), lambda g, i: (0, 0))],
        out_specs=pl.BlockSpec((c, w), lambda g, i: (i, g)),
        out_shape=jax.ShapeDtypeStruct((t, D_MODEL), BF16),
        scratch_shapes=[pltpu.VMEM((HG_HEADS_PER_STEP, HG_DIM, HG_DIM), F32)],
        compiler_params=_params(2),
        name="hgrn",
    )(main, main, main, main, hg_lb, hg_norm_w)


def _merge_body(ys_ref, yh_ref, ws_ref, wh_ref, gs_ref, gh_ref, o_ref):
    ps = jnp.dot(ys_ref[...], ws_ref[...], preferred_element_type=F32)
    ph = jnp.dot(yh_ref[...], wh_ref[...], preferred_element_type=F32)
    o_ref[...] = (_sigmoid(gs_ref[...]) * ps + _sigmoid(gh_ref[...]) * ph).astype(BF16)


def _merge(y_ssd, y_hg, w_s, w_h, main, tm=512, tn=1024):
    t = y_ssd.shape[0]
    tm = min(tm, t)
    per = D_MODEL // tn
    return pl.pallas_call(
        _merge_body,
        grid=(t // tm, D_MODEL // tn),
        in_specs=[pl.BlockSpec((tm, D_MODEL), lambda i, j: (i, 0)),
                  pl.BlockSpec((tm, D_MODEL), lambda i, j: (i, 0)),
                  pl.BlockSpec((D_MODEL, tn), lambda i, j: (0, j)),
                  pl.BlockSpec((D_MODEL, tn), lambda i, j: (0, j)),
                  pl.BlockSpec((tm, tn), lambda i, j: (i, COL_GS * per + j)),
                  pl.BlockSpec((tm, tn), lambda i, j: (i, COL_GH * per + j))],
        out_specs=pl.BlockSpec((tm, tn), lambda i, j: (i, j)),
        out_shape=jax.ShapeDtypeStruct((t, D_MODEL), BF16),
        compiler_params=_params(2),
        name="merge",
    )(y_ssd, y_hg, w_s, w_h, main, main)


def _outproj_body(mix_ref, w_ref, x_ref, nw_ref, x1_ref, h_ref, hb_ref):
    x1 = x_ref[...] + jnp.dot(mix_ref[...], w_ref[...], preferred_element_type=F32)
    x1_ref[...] = x1
    ms = jnp.mean(x1 * x1, axis=-1, keepdims=True)
    h = x1 * lax.rsqrt(ms + NORM_EPS) * nw_ref[...]
    h_ref[...] = h
    hb_ref[...] = h.astype(BF16)


def _outproj(mix, w_out, x, norm_w, tm=512):
    t = x.shape[0]
    tm = min(tm, t)
    row = pl.BlockSpec((tm, D_MODEL), lambda i: (i, 0))
    return pl.pallas_call(
        _outproj_body,
        grid=(t // tm,),
        in_specs=[row, pl.BlockSpec((D_MODEL, D_MODEL), lambda i: (0, 0)), row,
                  pl.BlockSpec((1, D_MODEL), lambda i: (0, 0))],
        out_specs=[row, row, row],
        out_shape=[jax.ShapeDtypeStruct((t, D_MODEL), F32),
                   jax.ShapeDtypeStruct((t, D_MODEL), F32),
                   jax.ShapeDtypeStruct((t, D_MODEL), BF16)],
        compiler_params=_params(1),
        name="outproj",
    )(mix, w_out, x, norm_w)


def _matmul_body(a_ref, b_ref, o_ref):
    o_ref[...] = jnp.dot(a_ref[...], b_ref[...], preferred_element_type=F32)


def _matmul(a, b, tm=512, tn=1024):
    m, k = a.shape
    n = b.shape[1]
    tm = min(tm, m)
    return pl.pallas_call(
        _matmul_body,
        grid=(m // tm, n // tn),
        in_specs=[pl.BlockSpec((tm, k), lambda i, j: (i, 0)),
                  pl.BlockSpec((k, tn), lambda i, j: (0, j))],
        out_specs=pl.BlockSpec((tm, tn), lambda i, j: (i, j)),
        out_shape=jax.ShapeDtypeStruct((m, n), F32),
        compiler_params=_params(2),
        name="peer_query",
    )(a, b)


def _topk_rows(s, k, on_pick):
    n = float(s.shape[0])
    rows = lax.broadcasted_iota(jnp.int32, s.shape, 0).astype(F32)
    for j in range(k):
        m = jnp.max(s, axis=0, keepdims=True)
        i = jnp.min(jnp.where(s == m, rows, n), axis=0, keepdims=True)
        hit = rows == i
        on_pick(j, m, i, hit)
        s = jnp.where(hit, -jnp.inf, s)


def _route_body(q_ref, keys_ref, idx_ref, gate_ref, s2_scr, i2_scr, ts_scr, cs_scr, ci_scr):
    k = PEER_TOPK
    q = q_ref[...]
    hi = lax.Precision.HIGHEST
    s_a = lax.dot_general(keys_ref[0, 0], q[:, :PEER_D_HALF], (((1,), (1,)), ((), ())),
                          precision=hi, preferred_element_type=F32)
    s_b = lax.dot_general(keys_ref[0, 1], q[:, PEER_D_HALF:], (((1,), (1,)), ((), ())),
                          precision=hi, preferred_element_type=F32)

    s1, i1 = [None] * k, [None] * k

    def pick_a(j, m, i, hit):
        s1[j], i1[j] = m, i

    def pick_b(j, m, i, hit):
        s2_scr[j:j + 1, :] = m
        i2_scr[j:j + 1, :] = i

    _topk_rows(s_a, k, pick_a)
    _topk_rows(s_b, k, pick_b)

    cs_scr[PEER_CAND_ROWS - SUBLANES:, :] = jnp.full((SUBLANES, cs_scr.shape[1]), -jnp.inf, F32)
    ci_scr[PEER_CAND_ROWS - SUBLANES:, :] = jnp.full((SUBLANES, ci_scr.shape[1]), -1.0, F32)
    off = 0
    for a in range(k):
        nb = k // (a + 1)
        cs_scr[off:off + nb, :] = s1[a] + s2_scr[0:nb, :]
        ci_scr[off:off + nb, :] = i1[a] * float(PEER_N_KEYS) + i2_scr[0:nb, :]
        off += nb
    cand_i = ci_scr[...]

    def pick_c(j, m, i, hit):
        ts_scr[j:j + 1, :] = m
        idx_ref[j:j + 1, :] = jnp.max(jnp.where(hit, cand_i, -1.0), axis=0, keepdims=True).astype(jnp.int32)

    _topk_rows(cs_scr[...], k, pick_c)
    ts = ts_scr[...]
    e = jnp.exp(ts - ts[0:1, :])
    gate_ref[...] = e / jnp.sum(e, axis=0, keepdims=True)


def _route(q, sub_keys, tb=PEER_ROUTE_TOKENS):
    t = q.shape[0]
    tb = min(tb, t)
    k = PEER_TOPK
    out_spec = pl.BlockSpec((k, tb), lambda i, h: (h, i))
    return pl.pallas_call(
        _route_body,
        grid=(t // tb, PEER_HEADS),
        in_specs=[pl.BlockSpec((tb, 2 * PEER_D_HALF), lambda i, h: (i, h)),
                  pl.BlockSpec((1, 2, PEER_N_KEYS, PEER_D_HALF), lambda i, h: (h, 0, 0, 0))],
        out_specs=[out_spec, out_spec],
        out_shape=[jax.ShapeDtypeStruct((PEER_PAIRS, t), jnp.int32),
                   jax.ShapeDtypeStruct((PEER_PAIRS, t), F32)],
        scratch_shapes=[pltpu.VMEM((k, tb), F32), pltpu.VMEM((k, tb), F32),
                        pltpu.VMEM((k, tb), F32),
                        pltpu.VMEM((PEER_CAND_ROWS, tb), F32), pltpu.VMEM((PEER_CAND_ROWS, tb), F32)],
        compiler_params=_params(2),
        name="peer_route",
    )(q, sub_keys)


def _peer_body(idx_ref, h_ref, x1_ref, gate_ref, fw_ref, tab_ref, o_ref, buf, acc_scr, uacc_scr, vacc_scr,
               wc_scr, sem):
    tb = h_ref.shape[0]
    step = pl.program_id(0)
    n_lane_tiles = D_MODEL // LANES
    half = PEER_PAIRS // 2
    n_part = half // n_lane_tiles
    part = PEER_PAIRS // n_part
    width = D_MODEL // n_part

    def start_row(tok, slot, r):
        e = idx_ref[tok, r]
        pltpu.make_async_copy(tab_ref.at[e], buf.at[slot, pl.ds(r, 1)], sem.at[slot]).start(priority=r % 2)

    def wait_token(slot):
        pltpu.make_async_copy(buf.at[slot], buf.at[slot], sem.at[slot]).wait()

    @pl.when(step == 0)
    def _():
        def first(p, carry):
            for r in range(PEER_PAIRS):
                start_row(p, p, r)
            return carry
        lax.fori_loop(0, PEER_AHEAD, first, 0)

    eye = (lax.broadcasted_iota(jnp.int32, (PEER_PAIRS, PEER_PAIRS), 0)
           == lax.broadcasted_iota(jnp.int32, (PEER_PAIRS, PEER_PAIRS), 1))
    hi_mask = jnp.uint32(0xFFFF0000)
    sub_iota = lax.broadcasted_iota(jnp.int32, (SUBLANES, D_MODEL), 0)
    acc_scr[...] = jnp.zeros(acc_scr.shape, F32)

    def u_pass(tok, pos):
        slot = pos % PEER_SLOTS
        nslot = (pos + PEER_AHEAD) % PEER_SLOTS
        wait_token(slot)
        hrow = h_ref[pl.ds(tok, 1), :]
        for j in range(n_lane_tiles):
            hj = hrow[:, j * LANES:(j + 1) * LANES]
            for p in range(n_part):
                start_row(tok + PEER_AHEAD, nslot, j * n_part + p)
                rows = slice(p * part, (p + 1) * part)
                prod = lax.bitcast_convert_type(buf[slot, rows, j * LANES:(j + 1) * LANES] & hi_mask, F32) * hj
                uacc_scr[rows, :] = prod if j == 0 else uacc_scr[rows, :] + prod
        act = jnp.sum(uacc_scr[...], axis=-1, keepdims=True)
        act = 0.5 * act * (1.0 + lax.erf(act * (2.0 ** -0.5)))
        grow = jnp.broadcast_to(gate_ref[pl.ds(tok, 1), :], (PEER_PAIRS, PEER_PAIRS))
        g = jnp.sum(jnp.where(eye, grow, 0.0), axis=-1, keepdims=True)
        return jnp.broadcast_to(act * g, (PEER_PAIRS, LANES))

    def v_pass(tok, pos, wcol):
        slot = pos % PEER_SLOTS
        nslot = (pos + PEER_AHEAD) % PEER_SLOTS
        for i in range(PEER_PAIRS // SUBLANES):
            rows = slice(i * SUBLANES, (i + 1) * SUBLANES)
            wblk = jnp.concatenate([wcol[rows, :]] * (width // LANES), axis=1)
            for p in range(n_part):
                start_row(tok + PEER_AHEAD, nslot, half + i * n_part + p)
                cols = slice(p * width, (p + 1) * width)
                prod = lax.bitcast_convert_type(buf[slot, rows, cols] << 16, F32) * wblk
                vacc_scr[:, cols] = prod if i == 0 else vacc_scr[:, cols] + prod
        orow = jnp.sum(vacc_scr[...], axis=0, keepdims=True)
        base = tok - pos % SUBLANES
        if not isinstance(base, int):
            base = pl.multiple_of(base, SUBLANES)
        tile = acc_scr[pl.ds(base, SUBLANES), :]
        acc_scr[pl.ds(base, SUBLANES), :] = jnp.where(sub_iota == pos % SUBLANES, orow, tile)

    def group(base, w_prev, first):
        for pos in range(PEER_SLOTS):
            w_cur = u_pass(base + pos, pos)
            if not (first and pos == 0):
                v_pass(base + pos - 1, pos - 1, w_prev)
            w_prev = w_cur
        return w_prev

    wc_scr[...] = group(0, None, True)

    def groups(g, carry):
        wc_scr[...] = group(pl.multiple_of(g * PEER_SLOTS, PEER_SLOTS), wc_scr[...], False)
        return carry

    lax.fori_loop(1, tb // PEER_SLOTS, groups, 0)
    v_pass(tb - 1, tb - 1, wc_scr[...])

    @pl.when(step == pl.num_programs(0) - 1)
    def _():
        for p in range(PEER_AHEAD):
            wait_token(p)

    xo = x1_ref[...] + acc_scr[...]
    ms = jnp.mean(xo * xo, axis=-1, keepdims=True)
    o_ref[...] = xo * lax.rsqrt(ms + NORM_EPS) * fw_ref[...]


def _peer(idx, h, x1, gate, final_w, table, tb=PEER_TOKEN_BLOCK):
    t = h.shape[0]
    n_steps = t // tb
    assert tb % PEER_SLOTS == 0 and 2 <= PEER_AHEAD < PEER_SLOTS - 1 and PEER_AHEAD <= tb
    blocks = idx.reshape(n_steps, tb, PEER_PAIRS)
    heads = jnp.roll(blocks[:, :PEER_SLOTS], -1, axis=0)
    idx = jnp.concatenate([blocks, heads], axis=1).reshape(n_steps * (tb + PEER_SLOTS), PEER_PAIRS)
    row = pl.BlockSpec((tb, D_MODEL), lambda i: (i, 0))
    return pl.pallas_call(
        _peer_body,
        grid=(n_steps,),
        in_specs=[pl.BlockSpec((tb + PEER_SLOTS, PEER_PAIRS), lambda i: (i, 0), memory_space=pltpu.SMEM),
                  row, row,
                  pl.BlockSpec((tb, PEER_PAIRS), lambda i: (i, 0)),
                  pl.BlockSpec((1, D_MODEL), lambda i: (0, 0)),
                  pl.BlockSpec(memory_space=pl.ANY)],
        out_specs=row,
        out_shape=jax.ShapeDtypeStruct((t, D_MODEL), F32),
        scratch_shapes=[pltpu.VMEM((PEER_SLOTS, PEER_PAIRS, D_MODEL), jnp.uint32),
                        pltpu.VMEM((tb, D_MODEL), F32),
                        pltpu.VMEM((PEER_PAIRS, LANES), F32),
                        pltpu.VMEM((SUBLANES, D_MODEL), F32),
                        pltpu.VMEM((PEER_PAIRS, LANES), F32),
                        pltpu.SemaphoreType.DMA((PEER_SLOTS,))],
        compiler_params=_params(1),
        name="peer_ffn",
    )(idx, h, x1, gate, final_w, table)


def _expand_heads(v, width):
    return jnp.repeat(v.astype(F32), width)[None, :]


def _layer(x, norm_mix_w, w_in, conv_w, conv_b, dt_bias, a_log, d_skip, ssd_norm_w, hg_lb, hg_norm_w,
           w_ssd_br, w_hg_br, w_out, norm_ffn_w, w_q, sub_keys, peer_u, peer_v, out_norm_w):
    d = D_MODEL
    conv_dim = conv_w.shape[1]
    o_dt = d + conv_dim
    w_dt = jnp.repeat(w_in[:, o_dt:o_dt + SSD_N_HEADS], SSD_HEAD_DIM, axis=1)
    w_main = jnp.concatenate([w_in[:, :o_dt], w_in[:, o_dt + SSD_N_HEADS:], w_dt], axis=1).astype(BF16)

    main = _inproj(x, norm_mix_w[None, :], w_main)
    y_ssd = _ssd(main, conv_w, conv_b[None, :], _expand_heads(dt_bias, SSD_HEAD_DIM),
                 _expand_heads(a_log, SSD_HEAD_DIM), _expand_heads(d_skip, SSD_HEAD_DIM),
                 ssd_norm_w[None, :])
    y_hg = _hgrn(main, hg_lb, hg_norm_w[None, :])
    mix = _merge(y_ssd, y_hg, w_ssd_br.astype(BF16), w_hg_br.astype(BF16), main)
    x1, h2, h2b = _outproj(mix, w_out.astype(BF16), x, norm_ffn_w[None, :])

    q = _matmul(h2b, w_q.astype(BF16))
    idx_t, gate_t = _route(q, sub_keys)
    ub = lax.bitcast_convert_type(peer_u.astype(BF16), jnp.uint16).astype(jnp.uint32)
    vb = lax.bitcast_convert_type(peer_v.astype(BF16), jnp.uint16).astype(jnp.uint32)
    table = ((ub << 16) | vb)[:, None, :]
    return _peer(idx_t.T, h2, x1, gate_t.T, out_norm_w[None, :], table)


def kernel(x, norm_mix_w, w_in, ssd_conv_w, ssd_conv_b, ssd_dt_bias, ssd_a_log, ssd_d, ssd_norm_w, hg_lb,
           hg_norm_w, w_ssd_br, w_hg_br, w_out, norm_ffn_w, peer_w_q, peer_sub_keys, peer_u, peer_v,
           final_norm_w):
    bsz, seq, d = x.shape
    depth = w_in.shape[0]
    assert bsz == 1 and depth == 1 and d == D_MODEL
    out = _layer(x.reshape(seq, d), norm_mix_w[0], w_in[0], ssd_conv_w[0], ssd_conv_b[0], ssd_dt_bias[0],
                 ssd_a_log[0], ssd_d[0], ssd_norm_w[0], hg_lb, hg_norm_w[0], w_ssd_br[0], w_hg_br[0],
                 w_out[0], norm_ffn_w[0], peer_w_q[0], peer_sub_keys[0], peer_u[0], peer_v[0], final_norm_w)
    return out.reshape(bsz, seq, d)
```

```python
import functools

import jax
import jax.numpy as jnp
from jax import lax
from jax.experimental import pallas as pl
from jax.experimental.pallas import tpu as pltpu

F32 = jnp.float32
BF16 = jnp.bfloat16

D_MODEL = 2048
NORM_EPS = 1e-6
LANES = 128
SUBLANES = 8
VMEM_LIMIT = 48 * 1024 * 1024

SSD_HEAD_DIM = 64
SSD_N_HEADS = 32
SSD_D_STATE = 128
SSD_N_GROUPS = 8
SSD_CONV_WIDTH = 4
SSD_CHUNK = 128
SSD_PAIRS = SSD_N_HEADS // 2
SSD_BC = SSD_N_GROUPS * SSD_D_STATE

HG_N_HEADS = 16
HG_DIM = 128
HG_CHUNK = 128
HG_HEADS_PER_STEP = 16
HG_INTERLEAVE = 8
HG_LEVELS = (64, 32, 16, 8, 4, 2, 1)

PEER_HEADS = 8
PEER_N_KEYS = 128
PEER_TOPK = 16
PEER_D_HALF = 128
PEER_PAIRS = PEER_HEADS * PEER_TOPK
PEER_ROUTE_TOKENS = 512
PEER_CAND_ROWS = -(-sum(PEER_TOPK // (a + 1) for a in range(PEER_TOPK)) // SUBLANES) * SUBLANES
PEER_TOKEN_BLOCK = 64
PEER_SLOTS = 8
PEER_AHEAD = 6

COL_Z, COL_XS, COL_BC, COL_HQ, COL_HF, COL_HI, COL_HG, COL_GS, COL_GH, COL_DT = range(10)
N_COL_BLOCKS = 10


def _params(n_axes, vmem=VMEM_LIMIT, flags=None):
    return pltpu.CompilerParams(dimension_semantics=("arbitrary",) * n_axes,
                                vmem_limit_bytes=vmem, flags=flags)


def _sigmoid(v):
    return 0.5 * jnp.tanh(0.5 * v) + 0.5


def _silu(v):
    return v * _sigmoid(v)


def _split3(v):
    hi = v.astype(BF16)
    r1 = v - hi.astype(F32)
    mid = r1.astype(BF16)
    lo = (r1 - mid.astype(F32)).astype(BF16)
    return hi, mid, lo


def _tri_cumsum(tri_bf16, v):
    hi, mid, lo = _split3(v)
    dot = functools.partial(jnp.dot, preferred_element_type=F32)
    return dot(tri_bf16, hi) + dot(tri_bf16, mid) + dot(tri_bf16, lo)


def _dot_nt(a, b):
    return lax.dot_general(a, b, (((1,), (1,)), ((), ())), preferred_element_type=F32)


def _lower_tri(n):
    row = lax.broadcasted_iota(jnp.int32, (n, n), 0)
    col = lax.broadcasted_iota(jnp.int32, (n, n), 1)
    return row >= col


def _inproj_body(x_ref, nw_ref, w_ref, o_ref, h_scr):
    @pl.when(pl.program_id(1) == 0)
    def _():
        x = x_ref[...]
        ms = jnp.mean(x * x, axis=-1, keepdims=True)
        h_scr[...] = (x * lax.rsqrt(ms + NORM_EPS) * nw_ref[...]).astype(BF16)

    o_ref[...] = jnp.dot(h_scr[...], w_ref[...], preferred_element_type=F32)


def _inproj(x, norm_w, w, tm=1024, tn=1024):
    t, d = x.shape
    n = w.shape[1]
    tm = min(tm, t)
    return pl.pallas_call(
        _inproj_body,
        grid=(t // tm, n // tn),
        in_specs=[pl.BlockSpec((tm, d), lambda i, j: (i, 0)),
                  pl.BlockSpec((1, d), lambda i, j: (0, 0)),
                  pl.BlockSpec((d, tn), lambda i, j: (0, j))],
        out_specs=pl.BlockSpec((tm, tn), lambda i, j: (i, j)),
        out_shape=jax.ShapeDtypeStruct((t, n), F32),
        scratch_shapes=[pltpu.VMEM((tm, d), BF16)],
        compiler_params=_params(2),
        name="inproj",
    )(x, norm_w, w)


def _ssd_body(z_ref, xs_ref, bc_ref, dt_ref, cw_ref, cb_ref, dtb_ref, alog_ref, dsk_ref, nw_ref,
              y_ref, ext_scr, xc_scr, yb_scr, st_scr):
    c = SSD_CHUNK
    halo = SUBLANES

    @pl.when(pl.program_id(0) == 0)
    def _():
        ext_scr[0:halo, :] = jnp.zeros((halo, ext_scr.shape[1]), F32)
        st_scr[...] = jnp.zeros(st_scr.shape, F32)

    ext_scr[halo:halo + c, 0:D_MODEL] = xs_ref[...]
    ext_scr[halo:halo + c, D_MODEL:] = bc_ref[...]

    cblk = 512
    for lo in range(0, ext_scr.shape[1], cblk):
        acc = cb_ref[:, lo:lo + cblk] + cw_ref[SSD_CONV_WIDTH - 1:SSD_CONV_WIDTH, lo:lo + cblk] * \
            ext_scr[halo:halo + c, lo:lo + cblk]
        for j in range(1, SSD_CONV_WIDTH):
            k = SSD_CONV_WIDTH - 1 - j
            acc = acc + cw_ref[k:k + 1, lo:lo + cblk] * ext_scr[halo - j:halo - j + c, lo:lo + cblk]
        xc_scr[:, lo:lo + cblk] = _silu(acc)
    ext_scr[0:halo, :] = ext_scr[c:c + halo, :]

    tri = _lower_tri(c)
    tri_b = tri.astype(BF16)
    lane = lax.broadcasted_iota(jnp.int32, (c, LANES), 1)
    lo_half = lane < SSD_HEAD_DIM

    dt_raw = dt_ref[...] + dtb_ref[...]
    dt = jnp.maximum(dt_raw, 0.0) + jnp.log(1.0 + jnp.exp(-jnp.abs(dt_raw)))
    adt = -jnp.exp(alog_ref[...]) * dt
    acum = _tri_cumsum(tri_b, adt)

    for pr in range(SSD_PAIRS):
        g = pr // 2
        sl = slice(pr * LANES, (pr + 1) * LANES)
        xs_p = xc_scr[:, sl]
        xdt = xs_p * dt[:, sl]
        ac = acum[:, sl]
        ac_t = ac.T
        if pr % 2 == 0:
            bm = xc_scr[:, D_MODEL + g * SSD_D_STATE:D_MODEL + (g + 1) * SSD_D_STATE]
            cm = xc_scr[:, D_MODEL + SSD_BC + g * SSD_D_STATE:D_MODEL + SSD_BC + (g + 1) * SSD_D_STATE]
            cm_b = cm.astype(BF16)
            cb = _dot_nt(cm_b, bm.astype(BF16))
            bm_t = bm.T.astype(BF16)
        gs = []
        for hh in range(2):
            col = jnp.broadcast_to(ac[:, hh * SSD_HEAD_DIM:hh * SSD_HEAD_DIM + 1], (c, c))
            row = ac_t[hh * SSD_HEAD_DIM:hh * SSD_HEAD_DIM + 1, :]
            decay = jnp.exp(jnp.where(tri, col - row, -1e30))
            gs.append(cb * decay)
        gmat = jnp.concatenate(gs, axis=1).astype(BF16)
        x2 = jnp.concatenate([jnp.where(lo_half, xdt, 0.0), jnp.where(lo_half, 0.0, xdt)],
                             axis=0).astype(BF16)
        y_diag = jnp.dot(gmat, x2, preferred_element_type=F32)
        st = st_scr[pr]
        y_off = jnp.dot(cm_b, st.astype(BF16), preferred_element_type=F32) * jnp.exp(ac)
        a_end = ac[c - 1:c, :]
        wgt = (xdt * jnp.exp(a_end - ac)).astype(BF16)
        st_scr[pr] = st * jnp.exp(a_end) + jnp.dot(bm_t, wgt, preferred_element_type=F32)
        y_p = y_diag + y_off + dsk_ref[:, sl] * xs_p
        yb_scr[:, sl] = y_p * _silu(z_ref[:, sl])

    gw = D_MODEL // SSD_N_GROUPS
    for g in range(SSD_N_GROUPS):
        yg = yb_scr[:, g * gw:(g + 1) * gw]
        ms = jnp.mean(yg * yg, axis=-1, keepdims=True)
        y_ref[:, g * gw:(g + 1) * gw] = (yg * lax.rsqrt(ms + NORM_EPS) * nw_ref[:, g * gw:(g + 1) * gw]).astype(BF16)


def _ssd(main, conv_w, conv_b, dtb_x, alog_x, dskip_x, norm_w):
    t = main.shape[0]
    c = SSD_CHUNK
    conv_dim = conv_w.shape[1]

    def col(b):
        return pl.BlockSpec((c, D_MODEL), lambda i, b=b: (i, b))

    def full(shape):
        return pl.BlockSpec(shape, lambda i: (0,) * len(shape))

    return pl.pallas_call(
        _ssd_body,
        grid=(t // c,),
        in_specs=[col(COL_Z), col(COL_XS), col(COL_BC), col(COL_DT),
                  full((SSD_CONV_WIDTH, conv_dim)), full((1, conv_dim)),
                  full((1, D_MODEL)), full((1, D_MODEL)), full((1, D_MODEL)), full((1, D_MODEL))],
        out_specs=pl.BlockSpec((c, D_MODEL), lambda i: (i, 0)),
        out_shape=jax.ShapeDtypeStruct((t, D_MODEL), BF16),
        scratch_shapes=[pltpu.VMEM((c + 2 * SUBLANES, conv_dim), F32),
                        pltpu.VMEM((c, conv_dim), F32),
                        pltpu.VMEM((c, D_MODEL), F32),
                        pltpu.VMEM((SSD_PAIRS, SSD_D_STATE, LANES), F32)],
        compiler_params=_params(1),
        name="ssd",
    )(main, main, main, main, conv_w, conv_b, dtb_x, alog_x, dskip_x, norm_w)


def _hg_level_operand(q, k, b, blk, is_tgt):
    n = b.shape[0]
    if blk >= SUBLANES:
        pieces = []
        for m in range(n // (2 * blk)):
            lo, mid, hi = 2 * m * blk, (2 * m + 1) * blk, (2 * m + 2) * blk
            anchor = b[mid - 1:mid, :]
            pieces.append(k[lo:mid, :] * jnp.exp(anchor - b[lo:mid, :]))
            pieces.append(q[mid:hi, :] * jnp.exp(b[mid:hi, :] - anchor))
        return jnp.concatenate(pieces, axis=0)
    if blk == SUBLANES // 2:
        anchor = jnp.concatenate(
            [jnp.broadcast_to(b[t * SUBLANES + blk - 1:t * SUBLANES + blk, :], (SUBLANES, b.shape[1]))
             for t in range(n // SUBLANES)], axis=0)
    else:
        pos = lax.broadcasted_iota(jnp.int32, b.shape, 0) & (2 * blk - 1)
        anchor = b
        for j in range(2 * blk):
            shift = j - (blk - 1)
            if shift != 0:
                anchor = jnp.where(pos == j, pltpu.roll(b, shift % n, 0), anchor)
    d = b - anchor
    return jnp.where(is_tgt, q, k) * jnp.exp(jnp.where(is_tgt, d, -d))


def _hgrn_body(hq_ref, hf_ref, hi_ref, hg_ref, lb_ref, nw_ref, y_ref, st_scr):
    c = HG_CHUNK

    @pl.when(pl.program_id(1) == 0)
    def _():
        st_scr[...] = jnp.zeros(st_scr.shape, F32)

    tri_b = _lower_tri(c).astype(BF16)
    row = lax.broadcasted_iota(jnp.int32, (c, c), 0)
    col = lax.broadcasted_iota(jnp.int32, (c, c), 1)
    differ = row ^ col
    below = row > col
    level_masks = []
    for blk in HG_LEVELS:
        sh = blk.bit_length() - 1
        is_tgt = ((row >> sh) & 1) == 1 if blk < SUBLANES else None
        level_masks.append((is_tgt, ((differ >> sh) == 1) & below))

    lbr = lb_ref[...]
    e = jnp.exp(lbr - jnp.max(lbr, axis=0, keepdims=True))
    lb_all = e[0:1, :] / jnp.sum(e, axis=0, keepdims=True)

    for h0 in range(0, HG_HEADS_PER_STEP, HG_INTERLEAVE):
        heads = range(h0, h0 + HG_INTERLEAVE)
        sls = {h: slice(h * HG_DIM, (h + 1) * HG_DIM) for h in heads}
        q, k, v, b, a, o = {}, {}, {}, {}, {}, {}
        for h in heads:
            q[h] = _silu(hq_ref[:, sls[h]])
            lb = lb_all[:, sls[h]]
            f = lb + (1.0 - lb) * _sigmoid(hf_ref[:, sls[h]])
            k[h] = 1.0 - f
            v[h] = hi_ref[:, sls[h]]
            b[h] = _tri_cumsum(tri_b, jnp.log(f))
            a[h] = jnp.zeros((c, c), F32)

        for blk, (is_tgt, pair_mask) in zip(HG_LEVELS, level_masks):
            for h in heads:
                xl = _hg_level_operand(q[h], k[h], b[h], blk, is_tgt).astype(BF16)
                a[h] = jnp.where(pair_mask, _dot_nt(xl, xl), a[h])
        for h in heads:
            o[h] = jnp.dot(a[h].astype(BF16), v[h].astype(BF16), preferred_element_type=F32)
            o[h] = o[h] + jnp.sum(q[h] * k[h], axis=-1, keepdims=True) * v[h]
        for h in heads:
            st = st_scr[h]
            o[h] = o[h] + _dot_nt((q[h] * jnp.exp(b[h])).astype(BF16), st.astype(BF16))
            b_end = b[h][c - 1:c, :]
            ke = (k[h] * jnp.exp(b_end - b[h])).astype(BF16)
            st_scr[h] = st * jnp.exp(b_end) + jnp.dot(v[h].T.astype(BF16), ke, preferred_element_type=F32)
        for h in heads:
            ms = jnp.mean(o[h] * o[h], axis=-1, keepdims=True)
            y_ref[:, sls[h]] = (o[h] * lax.rsqrt(ms + NORM_EPS) * nw_ref[...]
                                * _silu(hg_ref[:, sls[h]])).astype(BF16)


def _hgrn(main, hg_lb, hg_norm_w):
    t = main.shape[0]
    c = HG_CHUNK
    w = HG_HEADS_PER_STEP * HG_DIM
    per = D_MODEL // w

    def col(b):
        return pl.BlockSpec((c, w), lambda g, i, b=b: (i, b * per + g))

    return pl.pallas_call(
        _hgrn_body,
        grid=(per, t // c),
        in_specs=[col(COL_HQ), col(COL_HF), col(COL_HI), col(COL_HG),
                  pl.BlockSpec((hg_lb.shape[0], w), lambda g, i: (0, g)),
                  pl.BlockSpec((1, HG_DIM), lambda g, i: (0, 0))],
        out_specs=pl.BlockSpec((c, w), lambda g, i: (i, g)),
        out_shape=jax.ShapeDtypeStruct((t, D_MODEL), BF16),
        scratch_shapes=[pltpu.VMEM((HG_HEADS_PER_STEP, HG_DIM, HG_DIM), F32)],
        compiler_params=_params(2),
        name="hgrn",
    )(main, main, main, main, hg_lb, hg_norm_w)


def _merge_body(ys_ref, yh_ref, ws_ref, wh_ref, gs_ref, gh_ref, o_ref):
    ps = jnp.dot(ys_ref[...], ws_ref[...], preferred_element_type=F32)
    ph = jnp.dot(yh_ref[...], wh_ref[...], preferred_element_type=F32)
    o_ref[...] = (_sigmoid(gs_ref[...]) * ps + _sigmoid(gh_ref[...]) * ph).astype(BF16)


def _merge(y_ssd, y_hg, w_s, w_h, main, tm=512, tn=1024):
    t = y_ssd.shape[0]
    tm = min(tm, t)
    per = D_MODEL // tn
    return pl.pallas_call(
        _merge_body,
        grid=(t // tm, D_MODEL // tn),
        in_specs=[pl.BlockSpec((tm, D_MODEL), lambda i, j: (i, 0)),
                  pl.BlockSpec((tm, D_MODEL), lambda i, j: (i, 0)),
                  pl.BlockSpec((D_MODEL, tn), lambda i, j: (0, j)),
                  pl.BlockSpec((D_MODEL, tn), lambda i, j: (0, j)),
                  pl.BlockSpec((tm, tn), lambda i, j: (i, COL_GS * per + j)),
                  pl.BlockSpec((tm, tn), lambda i, j: (i, COL_GH * per + j))],
        out_specs=pl.BlockSpec((tm, tn), lambda i, j: (i, j)),
        out_shape=jax.ShapeDtypeStruct((t, D_MODEL), BF16),
        compiler_params=_params(2),
        name="merge",
    )(y_ssd, y_hg, w_s, w_h, main, main)


def _outproj_body(mix_ref, w_ref, x_ref, nw_ref, x1_ref, h_ref, hb_ref):
    x1 = x_ref[...] + jnp.dot(mix_ref[...], w_ref[...], preferred_element_type=F32)
    x1_ref[...] = x1
    ms = jnp.mean(x1 * x1, axis=-1, keepdims=True)
    h = x1 * lax.rsqrt(ms + NORM_EPS) * nw_ref[...]
    h_ref[...] = h
    hb_ref[...] = h.astype(BF16)


def _outproj(mix, w_out, x, norm_w, tm=512):
    t = x.shape[0]
    tm = min(tm, t)
    row = pl.BlockSpec((tm, D_MODEL), lambda i: (i, 0))
    return pl.pallas_call(
        _outproj_body,
        grid=(t // tm,),
        in_specs=[row, pl.BlockSpec((D_MODEL, D_MODEL), lambda i: (0, 0)), row,
                  pl.BlockSpec((1, D_MODEL), lambda i: (0, 0))],
        out_specs=[row, row, row],
        out_shape=[jax.ShapeDtypeStruct((t, D_MODEL), F32),
                   jax.ShapeDtypeStruct((t, D_MODEL), F32),
                   jax.ShapeDtypeStruct((t, D_MODEL), BF16)],
        compiler_params=_params(1),
        name="outproj",
    )(mix, w_out, x, norm_w)


def _matmul_body(a_ref, b_ref, o_ref):
    o_ref[...] = jnp.dot(a_ref[...], b_ref[...], preferred_element_type=F32)


def _matmul(a, b, tm=512, tn=1024):
    m, k = a.shape
    n = b.shape[1]
    tm = min(tm, m)
    return pl.pallas_call(
        _matmul_body,
        grid=(m // tm, n // tn),
        in_specs=[pl.BlockSpec((tm, k), lambda i, j: (i, 0)),
                  pl.BlockSpec((k, tn), lambda i, j: (0, j))],
        out_specs=pl.BlockSpec((tm, tn), lambda i, j: (i, j)),
        out_shape=jax.ShapeDtypeStruct((m, n), F32),
        compiler_params=_params(2),
        name="peer_query",
    )(a, b)


def _topk_rows(s, k, on_pick):
    n = float(s.shape[0])
    rows = lax.broadcasted_iota(jnp.int32, s.shape, 0).astype(F32)
    for j in range(k):
        m = jnp.max(s, axis=0, keepdims=True)
        i = jnp.min(jnp.where(s == m, rows, n), axis=0, keepdims=True)
        hit = rows == i
        on_pick(j, m, i, hit)
        s = jnp.where(hit, -jnp.inf, s)


def _route_body(q_ref, keys_ref, idx_ref, gate_ref, s2_scr, i2_scr, ts_scr, cs_scr, ci_scr):
    k = PEER_TOPK
    q = q_ref[...]
    hi = lax.Precision.HIGHEST
    s_a = lax.dot_general(keys_ref[0, 0], q[:, :PEER_D_HALF], (((1,), (1,)), ((), ())),
                          precision=hi, preferred_element_type=F32)
    s_b = lax.dot_general(keys_ref[0, 1], q[:, PEER_D_HALF:], (((1,), (1,)), ((), ())),
                          precision=hi, preferred_element_type=F32)

    s1, i1 = [None] * k, [None] * k

    def pick_a(j, m, i, hit):
        s1[j], i1[j] = m, i

    def pick_b(j, m, i, hit):
        s2_scr[j:j + 1, :] = m
        i2_scr[j:j + 1, :] = i

    _topk_rows(s_a, k, pick_a)
    _topk_rows(s_b, k, pick_b)

    cs_scr[PEER_CAND_ROWS - SUBLANES:, :] = jnp.full((SUBLANES, cs_scr.shape[1]), -jnp.inf, F32)
    ci_scr[PEER_CAND_ROWS - SUBLANES:, :] = jnp.full((SUBLANES, ci_scr.shape[1]), -1.0, F32)
    off = 0
    for a in range(k):
        nb = k // (a + 1)
        cs_scr[off:off + nb, :] = s1[a] + s2_scr[0:nb, :]
        ci_scr[off:off + nb, :] = i1[a] * float(PEER_N_KEYS) + i2_scr[0:nb, :]
        off += nb
    cand_i = ci_scr[...]

    def pick_c(j, m, i, hit):
        ts_scr[j:j + 1, :] = m
        idx_ref[j:j + 1, :] = jnp.max(jnp.where(hit, cand_i, -1.0), axis=0, keepdims=True).astype(jnp.int32)

    _topk_rows(cs_scr[...], k, pick_c)
    ts = ts_scr[...]
    e = jnp.exp(ts - ts[0:1, :])
    gate_ref[...] = e / jnp.sum(e, axis=0, keepdims=True)


def _route(q, sub_keys, tb=PEER_ROUTE_TOKENS):
    t = q.shape[0]
    tb = min(tb, t)
    k = PEER_TOPK
    out_spec = pl.BlockSpec((k, tb), lambda i, h: (h, i))
    return pl.pallas_call(
        _route_body,
        grid=(t // tb, PEER_HEADS),
        in_specs=[pl.BlockSpec((tb, 2 * PEER_D_HALF), lambda i, h: (i, h)),
                  pl.BlockSpec((1, 2, PEER_N_KEYS, PEER_D_HALF), lambda i, h: (h, 0, 0, 0))],
        out_specs=[out_spec, out_spec],
        out_shape=[jax.ShapeDtypeStruct((PEER_PAIRS, t), jnp.int32),
                   jax.ShapeDtypeStruct((PEER_PAIRS, t), F32)],
        scratch_shapes=[pltpu.VMEM((k, tb), F32), pltpu.VMEM((k, tb), F32),
                        pltpu.VMEM((k, tb), F32),
                        pltpu.VMEM((PEER_CAND_ROWS, tb), F32), pltpu.VMEM((PEER_CAND_ROWS, tb), F32)],
        compiler_params=_params(2),
        name="peer_route",
    )(q, sub_keys)


def _peer_body(idx_ref, h_ref, x1_ref, gate_ref, fw_ref, tab_ref, o_ref, buf, acc_scr, uacc_scr, vacc_scr,
               wc_scr, sem):
    tb = h_ref.shape[0]
    step = pl.program_id(0)
    n_lane_tiles = D_MODEL // LANES
    half = PEER_PAIRS // 2
    n_part = half // n_lane_tiles
    part = PEER_PAIRS // n_part
    width = D_MODEL // n_part

    def start_row(tok, slot, r):
        e = idx_ref[tok, r]
        pltpu.make_async_copy(tab_ref.at[e], buf.at[slot, pl.ds(r, 1)], sem.at[slot]).start(priority=r % 2)

    def wait_token(slot):
        pltpu.make_async_copy(buf.at[slot], buf.at[slot], sem.at[slot]).wait()

    @pl.when(step == 0)
    def _():
        def first(p, carry):
            for r in range(PEER_PAIRS):
                start_row(p, p, r)
            return carry
        lax.fori_loop(0, PEER_AHEAD, first, 0)

    eye = (lax.broadcasted_iota(jnp.int32, (PEER_PAIRS, PEER_PAIRS), 0)
           == lax.broadcasted_iota(jnp.int32, (PEER_PAIRS, PEER_PAIRS), 1))
    hi_mask = jnp.uint32(0xFFFF0000)
    sub_iota = lax.broadcasted_iota(jnp.int32, (SUBLANES, D_MODEL), 0)
    acc_scr[...] = jnp.zeros(acc_scr.shape, F32)

    def u_pass(tok, pos):
        slot = pos % PEER_SLOTS
        nslot = (pos + PEER_AHEAD) % PEER_SLOTS
        wait_token(slot)
        hrow = h_ref[pl.ds(tok, 1), :]
        for j in range(n_lane_tiles):
            hj = hrow[:, j * LANES:(j + 1) * LANES]
            for p in range(n_part):
                start_row(tok + PEER_AHEAD, nslot, j * n_part + p)
                rows = slice(p * part, (p + 1) * part)
                prod = lax.bitcast_convert_type(buf[slot, rows, j * LANES:(j + 1) * LANES] & hi_mask, F32) * hj
                uacc_scr[rows, :] = prod if j == 0 else uacc_scr[rows, :] + prod
        act = jnp.sum(uacc_scr[...], axis=-1, keepdims=True)
        act = 0.5 * act * (1.0 + lax.erf(act * (2.0 ** -0.5)))
        grow = jnp.broadcast_to(gate_ref[pl.ds(tok, 1), :], (PEER_PAIRS, PEER_PAIRS))
        g = jnp.sum(jnp.where(eye, grow, 0.0), axis=-1, keepdims=True)
        return jnp.broadcast_to(act * g, (PEER_PAIRS, LANES))

    def v_pass(tok, pos, wcol):
        slot = pos % PEER_SLOTS
        nslot = (pos + PEER_AHEAD) % PEER_SLOTS
        for i in range(PEER_PAIRS // SUBLANES):
            rows = slice(i * SUBLANES, (i + 1) * SUBLANES)
            wblk = jnp.concatenate([wcol[rows, :]] * (width // LANES), axis=1)
            for p in range(n_part):
                start_row(tok + PEER_AHEAD, nslot, half + i * n_part + p)
                cols = slice(p * width, (p + 1) * width)
                prod = lax.bitcast_convert_type(buf[slot, rows, cols] << 16, F32) * wblk
                vacc_scr[:, cols] = prod if i == 0 else vacc_scr[:, cols] + prod
        orow = jnp.sum(vacc_scr[...], axis=0, keepdims=True)
        base = tok - pos % SUBLANES
        if not isinstance(base, int):
            base = pl.multiple_of(base, SUBLANES)
        tile = acc_scr[pl.ds(base, SUBLANES), :]
        acc_scr[pl.ds(base, SUBLANES), :] = jnp.where(sub_iota == pos % SUBLANES, orow, tile)

    def group(base, w_prev, first):
        for pos in range(PEER_SLOTS):
            w_cur = u_pass(base + pos, pos)
            if not (first and pos == 0):
                v_pass(base + pos - 1, pos - 1, w_prev)
            w_prev = w_cur
        return w_prev

    wc_scr[...] = group(0, None, True)

    def groups(g, carry):
        wc_scr[...] = group(pl.multiple_of(g * PEER_SLOTS, PEER_SLOTS), wc_scr[...], False)
        return carry

    lax.fori_loop(1, tb // PEER_SLOTS, groups, 0)
    v_pass(tb - 1, tb - 1, wc_scr[...])

    @pl.when(step == pl.num_programs(0) - 1)
    def _():
        for p in range(PEER_AHEAD):
            wait_token(p)

    xo = x1_ref[...] + acc_scr[...]
    ms = jnp.mean(xo * xo, axis=-1, keepdims=True)
    o_ref[...] = xo * lax.rsqrt(ms + NORM_EPS) * fw_ref[...]


def _peer(idx, h, x1, gate, final_w, table, tb=PEER_TOKEN_BLOCK):
    t = h.shape[0]
    n_steps = t // tb
    assert tb % PEER_SLOTS == 0 and 2 <= PEER_AHEAD < PEER_SLOTS - 1 and PEER_AHEAD <= tb
    blocks = idx.reshape(n_steps, tb, PEER_PAIRS)
    heads = jnp.roll(blocks[:, :PEER_SLOTS], -1, axis=0)
    idx = jnp.concatenate([blocks, heads], axis=1).reshape(n_steps * (tb + PEER_SLOTS), PEER_PAIRS)
    row = pl.BlockSpec((tb, D_MODEL), lambda i: (i, 0))
    return pl.pallas_call(
        _peer_body,
        grid=(n_steps,),
        in_specs=[pl.BlockSpec((tb + PEER_SLOTS, PEER_PAIRS), lambda i: (i, 0), memory_space=pltpu.SMEM),
                  row, row,
                  pl.BlockSpec((tb, PEER_PAIRS), lambda i: (i, 0)),
                  pl.BlockSpec((1, D_MODEL), lambda i: (0, 0)),
                  pl.BlockSpec(memory_space=pl.ANY)],
        out_specs=row,
        out_shape=jax.ShapeDtypeStruct((t, D_MODEL), F32),
        scratch_shapes=[pltpu.VMEM((PEER_SLOTS, PEER_PAIRS, D_MODEL), jnp.uint32),
                        pltpu.VMEM((tb, D_MODEL), F32),
                        pltpu.VMEM((PEER_PAIRS, LANES), F32),
                        pltpu.VMEM((SUBLANES, D_MODEL), F32),
                        pltpu.VMEM((PEER_PAIRS, LANES), F32),
                        pltpu.SemaphoreType.DMA((PEER_SLOTS,))],
        compiler_params=_params(1),
        name="peer_ffn",
    )(idx, h, x1, gate, final_w, table)


def _expand_heads(v, width):
    return jnp.repeat(v.astype(F32), width)[None, :]


def _layer(x, norm_mix_w, w_in, conv_w, conv_b, dt_bias, a_log, d_skip, ssd_norm_w, hg_lb, hg_norm_w,
           w_ssd_br, w_hg_br, w_out, norm_ffn_w, w_q, sub_keys, peer_u, peer_v, out_norm_w):
    d = D_MODEL
    conv_dim = conv_w.shape[1]
    o_dt = d + conv_dim
    w_dt = jnp.repeat(w_in[:, o_dt:o_dt + SSD_N_HEADS], SSD_HEAD_DIM, axis=1)
    w_main = jnp.concatenate([w_in[:, :o_dt], w_in[:, o_dt + SSD_N_HEADS:], w_dt], axis=1).astype(BF16)

    main = _inproj(x, norm_mix_w[None, :], w_main)
    y_ssd = _ssd(main, conv_w, conv_b[None, :], _expand_heads(dt_bias, SSD_HEAD_DIM),
                 _expand_heads(a_log, SSD_HEAD_DIM), _expand_heads(d_skip, SSD_HEAD_DIM),
                 ssd_norm_w[None, :])
    y_hg = _hgrn(main, hg_lb, hg_norm_w[None, :])
    mix = _merge(y_ssd, y_hg, w_ssd_br.astype(BF16), w_hg_br.astype(BF16), main)
    x1, h2, h2b = _outproj(mix, w_out.astype(BF16), x, norm_ffn_w[None, :])

    q = _matmul(h2b, w_q.astype(BF16))
    idx_t, gate_t = _route(q, sub_keys)
    ub = lax.bitcast_convert_type(peer_u.astype(BF16), jnp.uint16).astype(jnp.uint32)
    vb = lax.bitcast_convert_type(peer_v.astype(BF16), jnp.uint16).astype(jnp.uint32)
    table = ((ub << 16) | vb)[:, None, :]
    return _peer(idx_t.T, h2, x1, gate_t.T, out_norm_w[None, :], table)


def kernel(x, norm_mix_w, w_in, ssd_conv_w, ssd_conv_b, ssd_dt_bias, ssd_a_log, ssd_d, ssd_norm_w, hg_lb,
           hg_norm_w, w_ssd_br, w_hg_br, w_out, norm_ffn_w, peer_w_q, peer_sub_keys, peer_u, peer_v,
           final_norm_w):
    bsz, seq, d = x.shape
    depth = w_in.shape[0]
    assert bsz == 1 and depth == 1 and d == D_MODEL
    out = _layer(x.reshape(seq, d), norm_mix_w[0], w_in[0], ssd_conv_w[0], ssd_conv_b[0], ssd_dt_bias[0],
                 ssd_a_log[0], ssd_d[0], ssd_norm_w[0], hg_lb, hg_norm_w[0], w_ssd_br[0], w_hg_br[0],
                 w_out[0], norm_ffn_w[0], peer_w_q[0], peer_sub_keys[0], peer_u[0], peer_v[0], final_norm_w)
    return out.reshape(bsz, seq, d)
```

```python
import functools

import jax
import jax.numpy as jnp
from jax import lax
from jax.experimental import pallas as pl
from jax.experimental.pallas import tpu as pltpu

F32 = jnp.float32
BF16 = jnp.bfloat16

D_MODEL = 2048
NORM_EPS = 1e-6
LANES = 128
SUBLANES = 8
VMEM_LIMIT = 48 * 1024 * 1024

SSD_HEAD_DIM = 64
SSD_N_HEADS = 32
SSD_D_STATE = 128
SSD_N_GROUPS = 8
SSD_CONV_WIDTH = 4
SSD_CHUNK = 128
SSD_PAIRS = SSD_N_HEADS // 2
SSD_BC = SSD_N_GROUPS * SSD_D_STATE

HG_N_HEADS = 16
HG_DIM = 128
HG_CHUNK = 128
HG_HEADS_PER_STEP = 16
HG_INTERLEAVE = 8
HG_LEVELS = (64, 32, 16, 8, 4, 2, 1)

PEER_HEADS = 8
PEER_N_KEYS = 128
PEER_TOPK = 16
PEER_D_HALF = 128
PEER_PAIRS = PEER_HEADS * PEER_TOPK
PEER_ROUTE_TOKENS = 512
PEER_CAND_ROWS = -(-sum(PEER_TOPK // (a + 1) for a in range(PEER_TOPK)) // SUBLANES) * SUBLANES
PEER_TOKEN_BLOCK = 64
PEER_SLOTS = 8
PEER_AHEAD = 6
PEER_LINES = D_MODEL // LANES
PEER_PITCH = PEER_LINES + 1

COL_Z, COL_XS, COL_BC, COL_HQ, COL_HF, COL_HI, COL_HG, COL_GS, COL_GH, COL_DT = range(10)
N_COL_BLOCKS = 10


def _params(n_axes, vmem=VMEM_LIMIT, flags=None):
    return pltpu.CompilerParams(dimension_semantics=("arbitrary",) * n_axes,
                                vmem_limit_bytes=vmem, flags=flags)


def _sigmoid(v):
    return 0.5 * jnp.tanh(0.5 * v) + 0.5


def _silu(v):
    return v * _sigmoid(v)


def _split3(v):
    hi = v.astype(BF16)
    r1 = v - hi.astype(F32)
    mid = r1.astype(BF16)
    lo = (r1 - mid.astype(F32)).astype(BF16)
    return hi, mid, lo


def _tri_cumsum(tri_bf16, v):
    hi, mid, lo = _split3(v)
    dot = functools.partial(jnp.dot, preferred_element_type=F32)
    return dot(tri_bf16, hi) + dot(tri_bf16, mid) + dot(tri_bf16, lo)


def _dot_nt(a, b):
    return lax.dot_general(a, b, (((1,), (1,)), ((), ())), preferred_element_type=F32)


def _lower_tri(n):
    row = lax.broadcasted_iota(jnp.int32, (n, n), 0)
    col = lax.broadcasted_iota(jnp.int32, (n, n), 1)
    return row >= col


def _inproj_body(x_ref, nw_ref, w_ref, o_ref, h_scr):
    @pl.when(pl.program_id(1) == 0)
    def _():
        x = x_ref[...]
        ms = jnp.mean(x * x, axis=-1, keepdims=True)
        h_scr[...] = (x * lax.rsqrt(ms + NORM_EPS) * nw_ref[...]).astype(BF16)

    o_ref[...] = jnp.dot(h_scr[...], w_ref[...], preferred_element_type=F32)


def _inproj(x, norm_w, w, tm=1024, tn=1024):
    t, d = x.shape
    n = w.shape[1]
    tm = min(tm, t)
    return pl.pallas_call(
        _inproj_body,
        grid=(t // tm, n // tn),
        in_specs=[pl.BlockSpec((tm, d), lambda i, j: (i, 0)),
                  pl.BlockSpec((1, d), lambda i, j: (0, 0)),
                  pl.BlockSpec((d, tn), lambda i, j: (0, j))],
        out_specs=pl.BlockSpec((tm, tn), lambda i, j: (i, j)),
        out_shape=jax.ShapeDtypeStruct((t, n), F32),
        scratch_shapes=[pltpu.VMEM((tm, d), BF16)],
        compiler_params=_params(2),
        name="inproj",
    )(x, norm_w, w)


def _ssd_body(z_ref, xs_ref, bc_ref, dt_ref, cw_ref, cb_ref, dtb_ref, alog_ref, dsk_ref, nw_ref,
              y_ref, ext_scr, xc_scr, yb_scr, st_scr):
    c = SSD_CHUNK
    halo = SUBLANES

    @pl.when(pl.program_id(0) == 0)
    def _():
        ext_scr[0:halo, :] = jnp.zeros((halo, ext_scr.shape[1]), F32)
        st_scr[...] = jnp.zeros(st_scr.shape, F32)

    ext_scr[halo:halo + c, 0:D_MODEL] = xs_ref[...]
    ext_scr[halo:halo + c, D_MODEL:] = bc_ref[...]

    cblk = 512
    for lo in range(0, ext_scr.shape[1], cblk):
        acc = cb_ref[:, lo:lo + cblk] + cw_ref[SSD_CONV_WIDTH - 1:SSD_CONV_WIDTH, lo:lo + cblk] * \
            ext_scr[halo:halo + c, lo:lo + cblk]
        for j in range(1, SSD_CONV_WIDTH):
            k = SSD_CONV_WIDTH - 1 - j
            acc = acc + cw_ref[k:k + 1, lo:lo + cblk] * ext_scr[halo - j:halo - j + c, lo:lo + cblk]
        xc_scr[:, lo:lo + cblk] = _silu(acc)
    ext_scr[0:halo, :] = ext_scr[c:c + halo, :]

    tri = _lower_tri(c)
    tri_b = tri.astype(BF16)
    lane = lax.broadcasted_iota(jnp.int32, (c, LANES), 1)
    lo_half = lane < SSD_HEAD_DIM

    dt_raw = dt_ref[...] + dtb_ref[...]
    dt = jnp.maximum(dt_raw, 0.0) + jnp.log(1.0 + jnp.exp(-jnp.abs(dt_raw)))
    adt = -jnp.exp(alog_ref[...]) * dt
    acum = _tri_cumsum(tri_b, adt)

    for pr in range(SSD_PAIRS):
        g = pr // 2
        sl = slice(pr * LANES, (pr + 1) * LANES)
        xs_p = xc_scr[:, sl]
        xdt = xs_p * dt[:, sl]
        ac = acum[:, sl]
        ac_t = ac.T
        if pr % 2 == 0:
            bm = xc_scr[:, D_MODEL + g * SSD_D_STATE:D_MODEL + (g + 1) * SSD_D_STATE]
            cm = xc_scr[:, D_MODEL + SSD_BC + g * SSD_D_STATE:D_MODEL + SSD_BC + (g + 1) * SSD_D_STATE]
            cm_b = cm.astype(BF16)
            cb = _dot_nt(cm_b, bm.astype(BF16))
            bm_t = bm.T.astype(BF16)
        gs = []
        for hh in range(2):
            col = jnp.broadcast_to(ac[:, hh * SSD_HEAD_DIM:hh * SSD_HEAD_DIM + 1], (c, c))
            row = ac_t[hh * SSD_HEAD_DIM:hh * SSD_HEAD_DIM + 1, :]
            decay = jnp.exp(jnp.where(tri, col - row, -1e30))
            gs.append(cb * decay)
        gmat = jnp.concatenate(gs, axis=1).astype(BF16)
        x2 = jnp.concatenate([jnp.where(lo_half, xdt, 0.0), jnp.where(lo_half, 0.0, xdt)],
                             axis=0).astype(BF16)
        y_diag = jnp.dot(gmat, x2, preferred_element_type=F32)
        st = st_scr[pr]
        y_off = jnp.dot(cm_b, st.astype(BF16), preferred_element_type=F32) * jnp.exp(ac)
        a_end = ac[c - 1:c, :]
        wgt = (xdt * jnp.exp(a_end - ac)).astype(BF16)
        st_scr[pr] = st * jnp.exp(a_end) + jnp.dot(bm_t, wgt, preferred_element_type=F32)
        y_p = y_diag + y_off + dsk_ref[:, sl] * xs_p
        yb_scr[:, sl] = y_p * _silu(z_ref[:, sl])

    gw = D_MODEL // SSD_N_GROUPS
    for g in range(SSD_N_GROUPS):
        yg = yb_scr[:, g * gw:(g + 1) * gw]
        ms = jnp.mean(yg * yg, axis=-1, keepdims=True)
        y_ref[:, g * gw:(g + 1) * gw] = (yg * lax.rsqrt(ms + NORM_EPS) * nw_ref[:, g * gw:(g + 1) * gw]).astype(BF16)


def _ssd(main, conv_w, conv_b, dtb_x, alog_x, dskip_x, norm_w):
    t = main.shape[0]
    c = SSD_CHUNK
    conv_dim = conv_w.shape[1]

    def col(b):
        return pl.BlockSpec((c, D_MODEL), lambda i, b=b: (i, b))

    def full(shape):
        return pl.BlockSpec(shape, lambda i: (0,) * len(shape))

    return pl.pallas_call(
        _ssd_body,
        grid=(t // c,),
        in_specs=[col(COL_Z), col(COL_XS), col(COL_BC), col(COL_DT),
                  full((SSD_CONV_WIDTH, conv_dim)), full((1, conv_dim)),
                  full((1, D_MODEL)), full((1, D_MODEL)), full((1, D_MODEL)), full((1, D_MODEL))],
        out_specs=pl.BlockSpec((c, D_MODEL), lambda i: (i, 0)),
        out_shape=jax.ShapeDtypeStruct((t, D_MODEL), BF16),
        scratch_shapes=[pltpu.VMEM((c + 2 * SUBLANES, conv_dim), F32),
                        pltpu.VMEM((c, conv_dim), F32),
                        pltpu.VMEM((c, D_MODEL), F32),
                        pltpu.VMEM((SSD_PAIRS, SSD_D_STATE, LANES), F32)],
        compiler_params=_params(1),
        name="ssd",
    )(main, main, main, main, conv_w, conv_b, dtb_x, alog_x, dskip_x, norm_w)


def _hg_level_operand(q, k, b, blk, is_tgt):
    n = b.shape[0]
    if blk >= SUBLANES:
        pieces = []
        for m in range(n // (2 * blk)):
            lo, mid, hi = 2 * m * blk, (2 * m + 1) * blk, (2 * m + 2) * blk
            anchor = b[mid - 1:mid, :]
            pieces.append(k[lo:mid, :] * jnp.exp(anchor - b[lo:mid, :]))
            pieces.append(q[mid:hi, :] * jnp.exp(b[mid:hi, :] - anchor))
        return jnp.concatenate(pieces, axis=0)
    if blk == SUBLANES // 2:
        anchor = jnp.concatenate(
            [jnp.broadcast_to(b[t * SUBLANES + blk - 1:t * SUBLANES + blk, :], (SUBLANES, b.shape[1]))
             for t in range(n // SUBLANES)], axis=0)
    else:
        pos = lax.broadcasted_iota(jnp.int32, b.shape, 0) & (2 * blk - 1)
        anchor = b
        for j in range(2 * blk):
            shift = j - (blk - 1)
            if shift != 0:
                anchor = jnp.where(pos == j, pltpu.roll(b, shift % n, 0), anchor)
    d = b - anchor
    return jnp.where(is_tgt, q, k) * jnp.exp(jnp.where(is_tgt, d, -d))


def _hgrn_body(hq_ref, hf_ref, hi_ref, hg_ref, lb_ref, nw_ref, y_ref, st_scr):
    c = HG_CHUNK

    @pl.when(pl.program_id(1) == 0)
    def _():
        st_scr[...] = jnp.zeros(st_scr.shape, F32)

    tri_b = _lower_tri(c).astype(BF16)
    row = lax.broadcasted_iota(jnp.int32, (c, c), 0)
    col = lax.broadcasted_iota(jnp.int32, (c, c), 1)
    differ = row ^ col
    below = row > col
    level_masks = []
    for blk in HG_LEVELS:
        sh = blk.bit_length() - 1
        is_tgt = ((row >> sh) & 1) == 1 if blk < SUBLANES else None
        level_masks.append((is_tgt, ((differ >> sh) == 1) & below))

    lbr = lb_ref[...]
    e = jnp.exp(lbr - jnp.max(lbr, axis=0, keepdims=True))
    lb_all = e[0:1, :] / jnp.sum(e, axis=0, keepdims=True)

    for h0 in range(0, HG_HEADS_PER_STEP, HG_INTERLEAVE):
        heads = range(h0, h0 + HG_INTERLEAVE)
        sls = {h: slice(h * HG_DIM, (h + 1) * HG_DIM) for h in heads}
        q, k, v, b, a, o = {}, {}, {}, {}, {}, {}
        for h in heads:
            q[h] = _silu(hq_ref[:, sls[h]])
            lb = lb_all[:, sls[h]]
            f = lb + (1.0 - lb) * _sigmoid(hf_ref[:, sls[h]])
            k[h] = 1.0 - f
            v[h] = hi_ref[:, sls[h]]
            b[h] = _tri_cumsum(tri_b, jnp.log(f))
            a[h] = jnp.zeros((c, c), F32)

        for blk, (is_tgt, pair_mask) in zip(HG_LEVELS, level_masks):
            for h in heads:
                xl = _hg_level_operand(q[h], k[h], b[h], blk, is_tgt).astype(BF16)
                a[h] = jnp.where(pair_mask, _dot_nt(xl, xl), a[h])
        for h in heads:
            o[h] = jnp.dot(a[h].astype(BF16), v[h].astype(BF16), preferred_element_type=F32)
            o[h] = o[h] + jnp.sum(q[h] * k[h], axis=-1, keepdims=True) * v[h]
        for h in heads:
            st = st_scr[h]
            o[h] = o[h] + _dot_nt((q[h] * jnp.exp(b[h])).astype(BF16), st.astype(BF16))
            b_end = b[h][c - 1:c, :]
            ke = (k[h] * jnp.exp(b_end - b[h])).astype(BF16)
            st_scr[h] = st * jnp.exp(b_end) + jnp.dot(v[h].T.astype(BF16), ke, preferred_element_type=F32)
        for h in heads:
            ms = jnp.mean(o[h] * o[h], axis=-1, keepdims=True)
            y_ref[:, sls[h]] = (o[h] * lax.rsqrt(ms + NORM_EPS) * nw_ref[...]
                                * _silu(hg_ref[:, sls[h]])).astype(BF16)


def _hgrn(main, hg_lb, hg_norm_w):
    t = main.shape[0]
    c = HG_CHUNK
    w = HG_HEADS_PER_STEP * HG_DIM
    per = D_MODEL // w

    def col(b):
        return pl.BlockSpec((c, w), lambda g, i, b=b: (i, b * per + g))

    return pl.pallas_call(
        _hgrn_body,
        grid=(per, t // c),
        in_specs=[col(COL_HQ), col(COL_HF), col(COL_HI), col(COL_HG),
                  pl.BlockSpec((hg_lb.shape[0], w), lambda g, i: (0, g)),
                  pl.BlockSpec((1, HG_DIM), lambda g, i: (0, 0))],
        out_specs=pl.BlockSpec((c, w), lambda g, i: (i, g)),
        out_shape=jax.ShapeDtypeStruct((t, D_MODEL), BF16),
        scratch_shapes=[pltpu.VMEM((HG_HEADS_PER_STEP, HG_DIM, HG_DIM), F32)],
        compiler_params=_params(2),
        name="hgrn",
    )(main, main, main, main, hg_lb, hg_norm_w)


def _merge_body(ys_ref, yh_ref, ws_ref, wh_ref, gs_ref, gh_ref, o_ref):
    ps = jnp.dot(ys_ref[...], ws_ref[...], preferred_element_type=F32)
    ph = jnp.dot(yh_ref[...], wh_ref[...], preferred_element_type=F32)
    o_ref[...] = (_sigmoid(gs_ref[...]) * ps + _sigmoid(gh_ref[...]) * ph).astype(BF16)


def _merge(y_ssd, y_hg, w_s, w_h, main, tm=512, tn=1024):
    t = y_ssd.shape[0]
    tm = min(tm, t)
    per = D_MODEL // tn
    return pl.pallas_call(
        _merge_body,
        grid=(t // tm, D_MODEL // tn),
        in_specs=[pl.BlockSpec((tm, D_MODEL), lambda i, j: (i, 0)),
                  pl.BlockSpec((tm, D_MODEL), lambda i, j: (i, 0)),
                  pl.BlockSpec((D_MODEL, tn), lambda i, j: (0, j)),
                  pl.BlockSpec((D_MODEL, tn), lambda i, j: (0, j)),
                  pl.BlockSpec((tm, tn), lambda i, j: (i, COL_GS * per + j)),
                  pl.BlockSpec((tm, tn), lambda i, j: (i, COL_GH * per + j))],
        out_specs=pl.BlockSpec((tm, tn), lambda i, j: (i, j)),
        out_shape=jax.ShapeDtypeStruct((t, D_MODEL), BF16),
        compiler_params=_params(2),
        name="merge",
    )(y_ssd, y_hg, w_s, w_h, main, main)


def _outproj_body(mix_ref, w_ref, x_ref, nw_ref, x1_ref, h_ref, hb_ref):
    x1 = x_ref[...] + jnp.dot(mix_ref[...], w_ref[...], preferred_element_type=F32)
    x1_ref[...] = x1
    ms = jnp.mean(x1 * x1, axis=-1, keepdims=True)
    h = x1 * lax.rsqrt(ms + NORM_EPS) * nw_ref[...]
    h_ref[...] = h
    hb_ref[...] = h.astype(BF16)


def _outproj(mix, w_out, x, norm_w, tm=512):
    t = x.shape[0]
    tm = min(tm, t)
    row = pl.BlockSpec((tm, D_MODEL), lambda i: (i, 0))
    return pl.pallas_call(
        _outproj_body,
        grid=(t // tm,),
        in_specs=[row, pl.BlockSpec((D_MODEL, D_MODEL), lambda i: (0, 0)), row,
                  pl.BlockSpec((1, D_MODEL), lambda i: (0, 0))],
        out_specs=[row, row, row],
        out_shape=[jax.ShapeDtypeStruct((t, D_MODEL), F32),
                   jax.ShapeDtypeStruct((t, D_MODEL), F32),
                   jax.ShapeDtypeStruct((t, D_MODEL), BF16)],
        compiler_params=_params(1),
        name="outproj",
    )(mix, w_out, x, norm_w)


def _matmul_body(a_ref, b_ref, o_ref):
    o_ref[...] = jnp.dot(a_ref[...], b_ref[...], preferred_element_type=F32)


def _matmul(a, b, tm=512, tn=1024):
    m, k = a.shape
    n = b.shape[1]
    tm = min(tm, m)
    return pl.pallas_call(
        _matmul_body,
        grid=(m // tm, n // tn),
        in_specs=[pl.BlockSpec((tm, k), lambda i, j: (i, 0)),
                  pl.BlockSpec((k, tn), lambda i, j: (0, j))],
        out_specs=pl.BlockSpec((tm, tn), lambda i, j: (i, j)),
        out_shape=jax.ShapeDtypeStruct((m, n), F32),
        compiler_params=_params(2),
        name="peer_query",
    )(a, b)


def _topk_rows(s, k, on_pick):
    n = float(s.shape[0])
    rows = lax.broadcasted_iota(jnp.int32, s.shape, 0).astype(F32)
    for j in range(k):
        m = jnp.max(s, axis=0, keepdims=True)
        i = jnp.min(jnp.where(s == m, rows, n), axis=0, keepdims=True)
        hit = rows == i
        on_pick(j, m, i, hit)
        s = jnp.where(hit, -jnp.inf, s)


def _route_body(q_ref, keys_ref, idx_ref, gate_ref, s2_scr, i2_scr, ts_scr, cs_scr, ci_scr):
    k = PEER_TOPK
    q = q_ref[...]
    hi = lax.Precision.HIGHEST
    s_a = lax.dot_general(keys_ref[0, 0], q[:, :PEER_D_HALF], (((1,), (1,)), ((), ())),
                          precision=hi, preferred_element_type=F32)
    s_b = lax.dot_general(keys_ref[0, 1], q[:, PEER_D_HALF:], (((1,), (1,)), ((), ())),
                          precision=hi, preferred_element_type=F32)

    s1, i1 = [None] * k, [None] * k

    def pick_a(j, m, i, hit):
        s1[j], i1[j] = m, i

    def pick_b(j, m, i, hit):
        s2_scr[j:j + 1, :] = m
        i2_scr[j:j + 1, :] = i

    _topk_rows(s_a, k, pick_a)
    _topk_rows(s_b, k, pick_b)

    cs_scr[PEER_CAND_ROWS - SUBLANES:, :] = jnp.full((SUBLANES, cs_scr.shape[1]), -jnp.inf, F32)
    ci_scr[PEER_CAND_ROWS - SUBLANES:, :] = jnp.full((SUBLANES, ci_scr.shape[1]), -1.0, F32)
    off = 0
    for a in range(k):
        nb = k // (a + 1)
        cs_scr[off:off + nb, :] = s1[a] + s2_scr[0:nb, :]
        ci_scr[off:off + nb, :] = i1[a] * float(PEER_N_KEYS) + i2_scr[0:nb, :]
        off += nb
    cand_i = ci_scr[...]

    def pick_c(j, m, i, hit):
        ts_scr[j:j + 1, :] = m
        idx_ref[j:j + 1, :] = jnp.max(jnp.where(hit, cand_i, -1.0), axis=0, keepdims=True).astype(jnp.int32)

    _topk_rows(cs_scr[...], k, pick_c)
    ts = ts_scr[...]
    e = jnp.exp(ts - ts[0:1, :])
    gate_ref[...] = e / jnp.sum(e, axis=0, keepdims=True)


def _route(q, sub_keys, tb=PEER_ROUTE_TOKENS):
    t = q.shape[0]
    tb = min(tb, t)
    k = PEER_TOPK
    out_spec = pl.BlockSpec((k, tb), lambda i, h: (h, i))
    return pl.pallas_call(
        _route_body,
        grid=(t // tb, PEER_HEADS),
        in_specs=[pl.BlockSpec((tb, 2 * PEER_D_HALF), lambda i, h: (i, h)),
                  pl.BlockSpec((1, 2, PEER_N_KEYS, PEER_D_HALF), lambda i, h: (h, 0, 0, 0))],
        out_specs=[out_spec, out_spec],
        out_shape=[jax.ShapeDtypeStruct((PEER_PAIRS, t), jnp.int32),
                   jax.ShapeDtypeStruct((PEER_PAIRS, t), F32)],
        scratch_shapes=[pltpu.VMEM((k, tb), F32), pltpu.VMEM((k, tb), F32),
                        pltpu.VMEM((k, tb), F32),
                        pltpu.VMEM((PEER_CAND_ROWS, tb), F32), pltpu.VMEM((PEER_CAND_ROWS, tb), F32)],
        compiler_params=_params(2),
        name="peer_route",
    )(q, sub_keys)


def _peer_body(idx_ref, h_ref, x1_ref, gate_ref, fw_ref, tab_ref, o_ref, buf, acc_scr, uacc_scr, vacc_scr,
               wc_scr, sem):
    tb = h_ref.shape[0]
    step = pl.program_id(0)
    n_lane_tiles = D_MODEL // LANES
    half = PEER_PAIRS // 2
    n_part = half // n_lane_tiles
    part = PEER_PAIRS // n_part
    width = D_MODEL // n_part

    def start_row(tok, slot, r):
        e = pl.multiple_of(idx_ref[tok, r] * PEER_LINES, PEER_LINES)
        pltpu.make_async_copy(tab_ref.at[pl.ds(e, PEER_LINES)], buf.at[slot, pl.ds(r * PEER_PITCH, PEER_LINES)],
                              sem.at[slot]).start(priority=r % 2)

    def wait_token(slot):
        span = buf.at[slot, pl.ds(0, PEER_PAIRS * PEER_LINES)]
        pltpu.make_async_copy(span, span, sem.at[slot]).wait()

    def row_tile(slot, i, j):
        return buf[slot, pl.ds(i * SUBLANES * PEER_PITCH + j, SUBLANES, stride=PEER_PITCH), :]

    @pl.when(step == 0)
    def _():
        def first(p, carry):
            for r in range(PEER_PAIRS):
                start_row(p, p, r)
            return carry
        lax.fori_loop(0, PEER_AHEAD, first, 0)

    eye = (lax.broadcasted_iota(jnp.int32, (PEER_PAIRS, PEER_PAIRS), 0)
           == lax.broadcasted_iota(jnp.int32, (PEER_PAIRS, PEER_PAIRS), 1))
    hi_mask = jnp.uint32(0xFFFF0000)
    sub_iota = lax.broadcasted_iota(jnp.int32, (SUBLANES, D_MODEL), 0)
    acc_scr[...] = jnp.zeros(acc_scr.shape, F32)

    def u_pass(tok, pos):
        slot = pos % PEER_SLOTS
        nslot = (pos + PEER_AHEAD) % PEER_SLOTS
        wait_token(slot)
        hrow = h_ref[pl.ds(tok, 1), :]
        for j in range(n_lane_tiles):
            hj = hrow[:, j * LANES:(j + 1) * LANES]
            for p in range(n_part):
                start_row(tok + PEER_AHEAD, nslot, j * n_part + p)
                rows = slice(p * part, (p + 1) * part)
                per = part // SUBLANES
                w = jnp.concatenate([row_tile(slot, p * per + i, j) for i in range(per)], axis=0)
                prod = lax.bitcast_convert_type(w & hi_mask, F32) * hj
                uacc_scr[rows, :] = prod if j == 0 else uacc_scr[rows, :] + prod
        act = jnp.sum(uacc_scr[...], axis=-1, keepdims=True)
        act = 0.5 * act * (1.0 + lax.erf(act * (2.0 ** -0.5)))
        grow = jnp.broadcast_to(gate_ref[pl.ds(tok, 1), :], (PEER_PAIRS, PEER_PAIRS))
        g = jnp.sum(jnp.where(eye, grow, 0.0), axis=-1, keepdims=True)
        return jnp.broadcast_to(act * g, (PEER_PAIRS, LANES))

    def v_pass(tok, pos, wcol):
        slot = pos % PEER_SLOTS
        nslot = (pos + PEER_AHEAD) % PEER_SLOTS
        for i in range(PEER_PAIRS // SUBLANES):
            rows = slice(i * SUBLANES, (i + 1) * SUBLANES)
            wblk = jnp.concatenate([wcol[rows, :]] * (width // LANES), axis=1)
            for p in range(n_part):
                start_row(tok + PEER_AHEAD, nslot, half + i * n_part + p)
                cols = slice(p * width, (p + 1) * width)
                per = width // LANES
                w = jnp.concatenate([row_tile(slot, i, p * per + j) for j in range(per)], axis=1)
                prod = lax.bitcast_convert_type(w << 16, F32) * wblk
                vacc_scr[:, cols] = prod if i == 0 else vacc_scr[:, cols] + prod
        orow = jnp.sum(vacc_scr[...], axis=0, keepdims=True)
        base = tok - pos % SUBLANES
        if not isinstance(base, int):
            base = pl.multiple_of(base, SUBLANES)
        tile = acc_scr[pl.ds(base, SUBLANES), :]
        acc_scr[pl.ds(base, SUBLANES), :] = jnp.where(sub_iota == pos % SUBLANES, orow, tile)

    def group(base, w_prev, first):
        for pos in range(PEER_SLOTS):
            w_cur = u_pass(base + pos, pos)
            if not (first and pos == 0):
                v_pass(base + pos - 1, pos - 1, w_prev)
            w_prev = w_cur
        return w_prev

    wc_scr[...] = group(0, None, True)

    def groups(g, carry):
        wc_scr[...] = group(pl.multiple_of(g * PEER_SLOTS, PEER_SLOTS), wc_scr[...], False)
        return carry

    lax.fori_loop(1, tb // PEER_SLOTS, groups, 0)
    v_pass(tb - 1, tb - 1, wc_scr[...])

    @pl.when(step == pl.num_programs(0) - 1)
    def _():
        for p in range(PEER_AHEAD):
            wait_token(p)

    xo = x1_ref[...] + acc_scr[...]
    ms = jnp.mean(xo * xo, axis=-1, keepdims=True)
    o_ref[...] = xo * lax.rsqrt(ms + NORM_EPS) * fw_ref[...]


def _peer(idx, h, x1, gate, final_w, table, tb=PEER_TOKEN_BLOCK):
    t = h.shape[0]
    n_steps = t // tb
    assert tb % PEER_SLOTS == 0 and 2 <= PEER_AHEAD < PEER_SLOTS - 1 and PEER_AHEAD <= tb
    blocks = idx.reshape(n_steps, tb, PEER_PAIRS)
    heads = jnp.roll(blocks[:, :PEER_SLOTS], -1, axis=0)
    idx = jnp.concatenate([blocks, heads], axis=1).reshape(n_steps * (tb + PEER_SLOTS), PEER_PAIRS)
    row = pl.BlockSpec((tb, D_MODEL), lambda i: (i, 0))
    return pl.pallas_call(
        _peer_body,
        grid=(n_steps,),
        in_specs=[pl.BlockSpec((tb + PEER_SLOTS, PEER_PAIRS), lambda i: (i, 0), memory_space=pltpu.SMEM),
                  row, row,
                  pl.BlockSpec((tb, PEER_PAIRS), lambda i: (i, 0)),
                  pl.BlockSpec((1, D_MODEL), lambda i: (0, 0)),
                  pl.BlockSpec(memory_space=pl.ANY)],
        out_specs=row,
        out_shape=jax.ShapeDtypeStruct((t, D_MODEL), F32),
        scratch_shapes=[pltpu.VMEM((PEER_SLOTS, PEER_PAIRS * PEER_PITCH, LANES), jnp.uint32),
                        pltpu.VMEM((tb, D_MODEL), F32),
                        pltpu.VMEM((PEER_PAIRS, LANES), F32),
                        pltpu.VMEM((SUBLANES, D_MODEL), F32),
                        pltpu.VMEM((PEER_PAIRS, LANES), F32),
                        pltpu.SemaphoreType.DMA((PEER_SLOTS,))],
        compiler_params=_params(1),
        name="peer_ffn",
    )(idx, h, x1, gate, final_w, table)


def _expand_heads(v, width):
    return jnp.repeat(v.astype(F32), width)[None, :]


def _layer(x, norm_mix_w, w_in, conv_w, conv_b, dt_bias, a_log, d_skip, ssd_norm_w, hg_lb, hg_norm_w,
           w_ssd_br, w_hg_br, w_out, norm_ffn_w, w_q, sub_keys, peer_u, peer_v, out_norm_w):
    d = D_MODEL
    conv_dim = conv_w.shape[1]
    o_dt = d + conv_dim
    w_dt = jnp.repeat(w_in[:, o_dt:o_dt + SSD_N_HEADS], SSD_HEAD_DIM, axis=1)
    w_main = jnp.concatenate([w_in[:, :o_dt], w_in[:, o_dt + SSD_N_HEADS:], w_dt], axis=1).astype(BF16)

    main = _inproj(x, norm_mix_w[None, :], w_main)
    y_ssd = _ssd(main, conv_w, conv_b[None, :], _expand_heads(dt_bias, SSD_HEAD_DIM),
                 _expand_heads(a_log, SSD_HEAD_DIM), _expand_heads(d_skip, SSD_HEAD_DIM),
                 ssd_norm_w[None, :])
    y_hg = _hgrn(main, hg_lb, hg_norm_w[None, :])
    mix = _merge(y_ssd, y_hg, w_ssd_br.astype(BF16), w_hg_br.astype(BF16), main)
    x1, h2, h2b = _outproj(mix, w_out.astype(BF16), x, norm_ffn_w[None, :])

    q = _matmul(h2b, w_q.astype(BF16))
    idx_t, gate_t = _route(q, sub_keys)
    ub = lax.bitcast_convert_type(peer_u.astype(BF16), jnp.uint16).astype(jnp.uint32)
    vb = lax.bitcast_convert_type(peer_v.astype(BF16), jnp.uint16).astype(jnp.uint32)
    table = ((ub << 16) | vb).reshape(-1, LANES)
    return _peer(idx_t.T, h2, x1, gate_t.T, out_norm_w[None, :], table)


def kernel(x, norm_mix_w, w_in, ssd_conv_w, ssd_conv_b, ssd_dt_bias, ssd_a_log, ssd_d, ssd_norm_w, hg_lb,
           hg_norm_w, w_ssd_br, w_hg_br, w_out, norm_ffn_w, peer_w_q, peer_sub_keys, peer_u, peer_v,
           final_norm_w):
    bsz, seq, d = x.shape
    depth = w_in.shape[0]
    assert bsz == 1 and depth == 1 and d == D_MODEL
    out = _layer(x.reshape(seq, d), norm_mix_w[0], w_in[0], ssd_conv_w[0], ssd_conv_b[0], ssd_dt_bias[0],
                 ssd_a_log[0], ssd_d[0], ssd_norm_w[0], hg_lb, hg_norm_w[0], w_ssd_br[0], w_hg_br[0],
                 w_out[0], norm_ffn_w[0], peer_w_q[0], peer_sub_keys[0], peer_u[0], peer_v[0], final_norm_w)
    return out.reshape(bsz, seq, d)
```
